```python
import jax, jax.numpy as jnp
from jax import lax
import numpy as np

D_MODEL = 1024
BATCH = 16
SEQ = 2048
DEPTH = 4
DEC_BATCH = 128
DEC_SEQ = 4
PAST_LEN = 8192
PAGE_SIZE = 128

D_MIX = D_MODEL
ATT_HEADS = 4
V_DIM = D_MIX // 8
ATT_WIDTH = ATT_HEADS * V_DIM
QK_NOPE = 128
QK_ROPE = 64
Q_RANK = D_MODEL // 4
KV_RANK = D_MODEL // 4
CONV_CH = D_MIX // 4
CONV_W = 31
CONV_PAST = CONV_W - 1
POOL_CH = D_MIX - ATT_WIDTH - CONV_CH
POOL_WINDOWS = (2, 4, 8, 16)
N_POOL_GROUPS = len(POOL_WINDOWS)
POOL_GC = POOL_CH // N_POOL_GROUPS
POOL_PAST = max(POOL_WINDOWS) - 1
IN_WIDTH = Q_RANK + KV_RANK + QK_ROPE + 2 * CONV_CH + POOL_CH
IN_SPLITS = (Q_RANK, Q_RANK + KV_RANK, Q_RANK + KV_RANK + QK_ROPE,
             Q_RANK + KV_RANK + QK_ROPE + 2 * CONV_CH)
ATT_SCALE = (QK_NOPE + QK_ROPE) ** -0.5
ROPE_BASE = 10000.0
Q_BLOCK = 128
D_FF = 2816
N_EXP = 8
TOP_K = 2
D_FF_EXP = 3584
N_DENSE = (DEPTH + 1) // 2
N_MOE = DEPTH // 2
ALPHA = (2 * DEPTH) ** 0.25
BETA = (8 * DEPTH) ** -0.25
LN_EPS = 1e-5
RMS_EPS = 1e-6

kernel_name = 'hymba_mla_conv_pool_deepnorm_step'


def layer_norm(x, g, b):
    xf = x.astype(jnp.float32)
    mu = jnp.mean(xf, axis=-1, keepdims=True)
    var = jnp.mean(jnp.square(xf - mu), axis=-1, keepdims=True)
    return ((xf - mu) * lax.rsqrt(var + LN_EPS)).astype(x.dtype) * g + b


def rms_norm(x, g):
    xf = x.astype(jnp.float32)
    return (xf * lax.rsqrt(jnp.mean(xf * xf, axis=-1, keepdims=True) + RMS_EPS)).astype(x.dtype) * g


def rope(x, pos):
    half = x.shape[-1] // 2
    inv = ROPE_BASE ** (-jnp.arange(half, dtype=jnp.float32) / half)
    ang = pos.astype(jnp.float32)[:, None] * inv[None, :]
    ang = ang.reshape(ang.shape[:1] + (1,) * (x.ndim - 3) + (half,))
    cos = jnp.cos(ang).astype(x.dtype)
    sin = jnp.sin(ang).astype(x.dtype)
    x1, x2 = x[..., :half], x[..., half:]
    return jnp.concatenate([x1 * cos - x2 * sin, x2 * cos + x1 * sin], axis=-1)


def latent_attention(q_lat, q_pe, ckv, kpe, q_pos, k_pos):
    s = (jnp.einsum('bqhc,bkc->bhqk', q_lat, ckv, preferred_element_type=jnp.float32)
         + jnp.einsum('bqhr,bkr->bhqk', q_pe, kpe, preferred_element_type=jnp.float32)) * ATT_SCALE
    s = jnp.where(k_pos[None, :] <= q_pos[:, None], s, -jnp.inf)
    prob = jax.nn.softmax(s, axis=-1).astype(ckv.dtype)
    return jnp.einsum('bhqk,bkc->bqhc', prob, ckv)


def prompt_attention(q_lat, q_pe, ckv, kpe):
    B, T, H, C = q_lat.shape
    nb = T // Q_BLOCK
    k_pos = jnp.arange(T, dtype=jnp.int32)

    def to_blocks(a):
        return jnp.moveaxis(a.reshape((B, nb, Q_BLOCK) + a.shape[2:]), 1, 0)

    def block(args):
        ql, qp, qpos = args
        return latent_attention(ql, qp, ckv, kpe, qpos, k_pos)

    o = lax.map(block, (to_blocks(q_lat), to_blocks(q_pe), k_pos.reshape(nb, Q_BLOCK)))
    return jnp.moveaxis(o, 0, 1).reshape(B, T, H, C)


def causal_depthwise_conv(u, prev, w, b):
    ext = jnp.concatenate([prev, u], axis=1)
    y = lax.conv_general_dilated(ext, w[:, None, :], window_strides=(1,), padding='VALID',
                                 dimension_numbers=('NWC', 'WIO', 'NWC'),
                                 feature_group_count=u.shape[-1])
    return y + b, ext[:, -CONV_PAST:]


def pool_mixer(v, prev, pos, pool_w, pool_scale):
    B, T, _ = v.shape
    ext = jnp.concatenate([prev, v], axis=1)
    cs = jnp.concatenate([jnp.zeros((B, 1, POOL_CH), jnp.float32),
                          jnp.cumsum(ext.astype(jnp.float32), axis=1)], axis=1)
    end = cs[:, POOL_PAST + 1:]
    means = []
    for g, w in enumerate(POOL_WINDOWS):
        sl = slice(g * POOL_GC, (g + 1) * POOL_GC)
        start = cs[:, POOL_PAST + 1 - w:POOL_PAST + 1 - w + T, sl]
        cnt = jnp.minimum(pos + 1, w).astype(jnp.float32)[None, :, None]
        means.append((end[..., sl] - start) / cnt)
    mean = jnp.concatenate(means, axis=-1).astype(v.dtype)
    d = (mean - v).reshape(B, T, N_POOL_GROUPS, POOL_GC)
    y = jnp.einsum('btgc,gcd->btgd', d, pool_w).reshape(B, T, POOL_CH) * pool_scale
    return y, ext[:, -POOL_PAST:]


def mixer_layer(x, prev_conv, prev_pool, past_ckv, past_kpe, p):
    B, T, _ = x.shape
    pos0 = 0 if past_ckv is None else past_ckv.shape[1]
    pos = pos0 + jnp.arange(T, dtype=jnp.int32)
    h = jnp.einsum('btd,dn->btn', x, p['w_in'])
    h_q, h_kv, h_pe, h_glu, h_pool = jnp.split(h, IN_SPLITS, axis=-1)
    q = jnp.einsum('btr,rhn->bthn', rms_norm(h_q, p['q_norm_g']), p['w_q_b'])
    q_pe = rope(q[..., QK_NOPE:], pos)
    q_lat = jnp.einsum('bthn,chn->bthc', q[..., :QK_NOPE], p['w_uk'])
    ckv = rms_norm(h_kv, p['kv_norm_g'])
    kpe = rope(h_pe, pos)
    if past_ckv is None:
        o_lat = prompt_attention(q_lat, q_pe, ckv, kpe)
    else:
        all_ckv = jnp.concatenate([past_ckv, ckv], axis=1)
        all_kpe = jnp.concatenate([past_kpe, kpe], axis=1)
        k_pos = jnp.arange(all_ckv.shape[1], dtype=jnp.int32)
        o_lat = latent_attention(q_lat, q_pe, all_ckv, all_kpe, pos, k_pos)
    o_att = jnp.einsum('bthc,chv->bthv', o_lat, p['w_uv']).reshape(B, T, ATT_WIDTH)
    u = h_glu[..., :CONV_CH] * jax.nn.sigmoid(h_glu[..., CONV_CH:])
    c, new_conv = causal_depthwise_conv(u, prev_conv, p['conv_w'], p['conv_b'])
    o_conv = jax.nn.silu(layer_norm(c, p['conv_ln_g'], p['conv_ln_b']))
    o_pool, new_pool = pool_mixer(h_pool, prev_pool, pos, p['pool_w'], p['pool_scale'])
    mix = jnp.concatenate([o_att, o_conv, o_pool], axis=-1) @ p['w_out']
    x = layer_norm(ALPHA * x + mix, p['ln1_g'], p['ln1_b'])
    return x, ckv, kpe, new_conv, new_pool


def swiglu(x, wg, wu, wd):
    return (jax.nn.silu(x @ wg) * (x @ wu)) @ wd


def moe_swiglu(x, w_router, b_router, wg, wu, wd):
    logits = (x @ w_router + b_router).astype(jnp.float32)
    top_val, top_idx = lax.top_k(logits, TOP_K)
    gates = jax.nn.softmax(top_val, axis=-1)
    combine = jnp.sum(jax.nn.one_hot(top_idx, N_EXP, dtype=jnp.float32) * gates[..., None],
                      axis=-2).astype(x.dtype)
    y = jnp.zeros_like(x)
    for e in range(N_EXP):
        y = y + combine[..., e:e + 1] * swiglu(x, wg[e], wu[e], wd[e])
    return y


def gather_pages(pool, page_table):
    g = jnp.take(pool, page_table, axis=0)
    return g.reshape(page_table.shape[0], -1, pool.shape[-1])


def setup_inputs(seed: int = 0) -> dict:
    key = jax.random.key(seed)
    keys = jax.random.split(key, 40)
    counter = iter(range(40))

    def nk():
        return keys[next(counter)]

    def nrm(shape, scale):
        return jax.random.normal(nk(), shape, jnp.float32) * scale

    def gain(shape):
        return 1.0 + 0.1 * jax.random.normal(nk(), shape, jnp.float32)

    n_pages = PAST_LEN // PAGE_SIZE
    n_used = DEC_BATCH * n_pages
    n_phys = n_used + max(1, n_used // 4)
    x_prompt = nrm((BATCH, SEQ, D_MODEL), 1.0)
    x_sample = nrm((DEC_BATCH, DEC_SEQ, D_MODEL), 1.0)
    cache_ckv = nrm((DEPTH, n_phys, PAGE_SIZE, KV_RANK), 1.0)
    cache_kpe = nrm((DEPTH, n_phys, PAGE_SIZE, QK_ROPE), 1.0)
    state_conv = nrm((DEPTH, DEC_BATCH, CONV_PAST, CONV_CH), 0.5)
    state_pool = nrm((DEPTH, DEC_BATCH, POOL_PAST, POOL_CH), 1.0)
    page_table = jax.random.permutation(nk(), n_phys)[:n_used].reshape(DEC_BATCH, n_pages).astype(jnp.int32)
    return {
        'x_prompt': x_prompt,
        'x_sample': x_sample,
        'cache_ckv': cache_ckv,
        'cache_kpe': cache_kpe,
        'state_conv': state_conv,
        'state_pool': state_pool,
        'page_table': page_table,
        'w_in': nrm((DEPTH, D_MODEL, IN_WIDTH), D_MODEL ** -0.5),
        'q_norm_g': gain((DEPTH, Q_RANK)),
        'kv_norm_g': gain((DEPTH, KV_RANK)),
        'w_q_b': nrm((DEPTH, Q_RANK, ATT_HEADS, QK_NOPE + QK_ROPE), Q_RANK ** -0.5),
        'w_uk': nrm((DEPTH, KV_RANK, ATT_HEADS, QK_NOPE), KV_RANK ** -0.5),
        'w_uv': nrm((DEPTH, KV_RANK, ATT_HEADS, V_DIM), KV_RANK ** -0.5),
        'conv_w': nrm((DEPTH, CONV_W, CONV_CH), CONV_W ** -0.5),
        'conv_b': nrm((DEPTH, CONV_CH), 0.02),
        'conv_ln_g': gain((DEPTH, CONV_CH)),
        'conv_ln_b': nrm((DEPTH, CONV_CH), 0.02),
        'pool_w': nrm((DEPTH, N_POOL_GROUPS, POOL_GC, POOL_GC), POOL_GC ** -0.5),
        'pool_scale': gain((DEPTH, POOL_CH)),
        'w_out': nrm((DEPTH, D_MIX, D_MODEL), BETA * D_MIX ** -0.5),
        'ln1_g': gain((DEPTH, D_MODEL)),
        'ln1_b': nrm((DEPTH, D_MODEL), 0.02),
        'w_ffn_gate': nrm((N_DENSE, D_MODEL, D_FF), D_MODEL ** -0.5),
        'w_ffn_up': nrm((N_DENSE, D_MODEL, D_FF), D_MODEL ** -0.5),
        'w_ffn_down': nrm((N_DENSE, D_FF, D_MODEL), BETA * D_FF ** -0.5),
        'w_router': nrm((N_MOE, D_MODEL, N_EXP), D_MODEL ** -0.5),
        'b_router': nrm((N_MOE, N_EXP), 0.01),
        'w_exp_gate': nrm((N_MOE, N_EXP, D_MODEL, D_FF_EXP), D_MODEL ** -0.5),
        'w_exp_up': nrm((N_MOE, N_EXP, D_MODEL, D_FF_EXP), D_MODEL ** -0.5),
        'w_exp_down': nrm((N_MOE, N_EXP, D_FF_EXP, D_MODEL), BETA * D_FF_EXP ** -0.5),
        'ln2_g': gain((DEPTH, D_MODEL)),
        'ln2_b': nrm((DEPTH, D_MODEL), 0.02),
    }


def reference(x_prompt, x_sample, cache_ckv, cache_kpe, state_conv, state_pool, page_table,
              w_in, q_norm_g, kv_norm_g, w_q_b, w_uk, w_uv, conv_w, conv_b, conv_ln_g, conv_ln_b,
              pool_w, pool_scale, w_out, ln1_g, ln1_b,
              w_ffn_gate, w_ffn_up, w_ffn_down, w_router, b_router,
              w_exp_gate, w_exp_up, w_exp_down, ln2_g, ln2_b):
    xp, xs = x_prompt, x_sample
    zero_conv = jnp.zeros((xp.shape[0], CONV_PAST, CONV_CH), xp.dtype)
    zero_pool = jnp.zeros((xp.shape[0], POOL_PAST, POOL_CH), xp.dtype)
    ckv_p, kpe_p, conv_p, pool_p = [], [], [], []
    ckv_s, kpe_s, conv_s, pool_s = [], [], [], []
    for l in range(DEPTH):
        p = {'w_in': w_in[l], 'q_norm_g': q_norm_g[l], 'kv_norm_g': kv_norm_g[l],
             'w_q_b': w_q_b[l], 'w_uk': w_uk[l], 'w_uv': w_uv[l],
             'conv_w': conv_w[l], 'conv_b': conv_b[l],
             'conv_ln_g': conv_ln_g[l], 'conv_ln_b': conv_ln_b[l],
             'pool_w': pool_w[l], 'pool_scale': pool_scale[l], 'w_out': w_out[l],
             'ln1_g': ln1_g[l], 'ln1_b': ln1_b[l]}
        xp, a, b, c, d = mixer_layer(xp, zero_conv, zero_pool, None, None, p)
        ckv_p.append(a); kpe_p.append(b); conv_p.append(c); pool_p.append(d)
        past_ckv = gather_pages(cache_ckv[l], page_table)
        past_kpe = gather_pages(cache_kpe[l], page_table)
        xs, a, b, c, d = mixer_layer(xs, state_conv[l], state_pool[l], past_ckv, past_kpe, p)
        ckv_s.append(a); kpe_s.append(b); conv_s.append(c); pool_s.append(d)
        i = l // 2
        if l % 2 == 0:
            fp = swiglu(xp, w_ffn_gate[i], w_ffn_up[i], w_ffn_down[i])
            fs = swiglu(xs, w_ffn_gate[i], w_ffn_up[i], w_ffn_down[i])
        else:
            fp = moe_swiglu(xp, w_router[i], b_router[i], w_exp_gate[i], w_exp_up[i], w_exp_down[i])
            fs = moe_swiglu(xs, w_router[i], b_router[i], w_exp_gate[i], w_exp_up[i], w_exp_down[i])
        xp = layer_norm(ALPHA * xp + fp, ln2_g[l], ln2_b[l])
        xs = layer_norm(ALPHA * xs + fs, ln2_g[l], ln2_b[l])
    return (xp, xs,
            jnp.stack(ckv_p), jnp.stack(kpe_p), jnp.stack(conv_p), jnp.stack(pool_p),
            jnp.stack(ckv_s), jnp.stack(kpe_s), jnp.stack(conv_s), jnp.stack(pool_s))
```

```python
import functools

import jax
import jax.numpy as jnp
from jax import lax
from jax.experimental import pallas as pl
from jax.experimental.pallas import tpu as pltpu

F32 = jnp.float32
BF16 = jnp.bfloat16

LN_EPS = 1e-5
RMS_EPS = 1e-6
ROPE_BASE = 10000.0
PAGE_SIZE = 128
TOP_K = 2
POOL_WINDOWS = (2, 4, 8, 16)

LANES = 128
V7X_VMEM_LIMIT = 56 * 1024 * 1024
TOKEN_TILE = 512
ATTN_TILE = 512
SEQ_TILE = 256
CONV_CHUNK = 64
DECODE_GROUP = 4
NEG_BIG = -1e30


def _cparams(sem):
    return pltpu.CompilerParams(dimension_semantics=sem, vmem_limit_bytes=V7X_VMEM_LIMIT)


def _dot(a, b):
    return jnp.dot(a, b, preferred_element_type=F32)


def _dot_nt(a, b):
    return lax.dot_general(a, b, (((1,), (1,)), ((), ())), preferred_element_type=F32)


def _lane_repeat(x, n):
    return x if n == 1 else jnp.concatenate([x] * n, axis=1)


def _sigmoid(x):
    return 1.0 / (1.0 + jnp.exp(-x))


def _layer_norm(y, g, b):
    mu = jnp.mean(y, axis=-1, keepdims=True)
    yc = y - mu
    var = jnp.mean(yc * yc, axis=-1, keepdims=True)
    return yc * lax.rsqrt(var + LN_EPS) * g + b


def _rms_norm(h, g):
    return h * lax.rsqrt(jnp.mean(h * h, axis=-1, keepdims=True) + RMS_EPS) * g


def _const_spec(shape):
    nd = len(shape)
    return pl.BlockSpec(shape, lambda *_: (0,) * nd)


def _resident_spec(shape):
    nd = len(shape)
    return pl.BlockSpec(shape, lambda *_: (0,) * nd, pipeline_mode=pl.Buffered(1))


def _in_proj_kernel(x_ref, win_ref, wq_ref, wuk_ref, qg_ref, kvg_ref, cos_ref, sin_ref,
                    qlat_ref, qpe_ref, ckv_ref, kpe_ref, ckvb_ref, kpeb_ref, u_ref, hp_ref,
                    *, heads, q_rank, kv_rank, conv_ch, pool_ch, nope, rope, scale):
    x = x_ref[...].astype(BF16)
    h = _dot(x, win_ref[...])
    o_kv = q_rank
    o_glu = o_kv + kv_rank
    o_pool = o_glu + 2 * conv_ch
    o_pe = o_pool + pool_ch
    o_sw = o_pe + LANES
    cos = cos_ref[...]
    sin = sin_ref[...]

    qn = _rms_norm(h[:, :q_rank], qg_ref[...]).astype(BF16)
    q = _dot(qn, wq_ref[...])
    o_qpe = heads * nope
    o_qsw = o_qpe + heads * LANES
    for hd in range(heads):
        q_nope = q[:, hd * nope:(hd + 1) * nope].astype(BF16)
        qlat_ref[hd] = (_dot(q_nope, wuk_ref[hd]) * scale).astype(BF16)
        pe = q[:, o_qpe + hd * LANES:o_qpe + (hd + 1) * LANES]
        sw = q[:, o_qsw + hd * LANES:o_qsw + (hd + 1) * LANES]
        qpe_ref[hd] = ((pe * cos + sw * sin) * scale).astype(BF16)

    ckv = _rms_norm(h[:, o_kv:o_kv + kv_rank], kvg_ref[...])
    ckv_ref[...] = ckv
    ckvb_ref[...] = ckv.astype(BF16)
    kpe = h[:, o_pe:o_pe + LANES] * cos + h[:, o_sw:o_sw + LANES] * sin
    kpe_ref[...] = kpe[:, :rope]
    kpeb_ref[...] = kpe.astype(BF16)

    u_ref[...] = h[:, o_glu:o_glu + conv_ch] * _sigmoid(h[:, o_glu + conv_ch:o_glu + 2 * conv_ch])
    hp_ref[...] = h[:, o_pool:o_pool + pool_ch]


def _in_proj(x, win, wq, wuk, qg, kvg, cos_tab, sin_tab, tab_index, dm):
    n, d = x.shape
    tm = TOKEN_TILE
    heads, q_rank, kv_rank = dm['heads'], dm['q_rank'], dm['kv_rank']
    conv_ch, pool_ch, rope = dm['conv_ch'], dm['pool_ch'], dm['rope']
    row = lambda i: (i, 0)
    hrow = lambda i: (0, i, 0)
    kern = functools.partial(_in_proj_kernel, heads=heads, q_rank=q_rank, kv_rank=kv_rank,
                             conv_ch=conv_ch, pool_ch=pool_ch, nope=dm['nope'], rope=rope,
                             scale=dm['att_scale'])
    return pl.pallas_call(
        kern,
        grid=(n // tm,),
        in_specs=[
            pl.BlockSpec((tm, d), row),
            _const_spec(win.shape), _const_spec(wq.shape), _const_spec(wuk.shape),
            _const_spec(qg.shape), _const_spec(kvg.shape),
            pl.BlockSpec((tm, LANES), lambda i: (tab_index(i), 0)),
            pl.BlockSpec((tm, LANES), lambda i: (tab_index(i), 0)),
        ],
        out_specs=[
            pl.BlockSpec((heads, tm, kv_rank), hrow),
            pl.BlockSpec((heads, tm, LANES), hrow),
            pl.BlockSpec((tm, kv_rank), row),
            pl.BlockSpec((tm, rope), row),
            pl.BlockSpec((tm, kv_rank), row),
            pl.BlockSpec((tm, LANES), row),
            pl.BlockSpec((tm, conv_ch), row),
            pl.BlockSpec((tm, pool_ch), row),
        ],
        out_shape=[
            jax.ShapeDtypeStruct((heads, n, kv_rank), BF16),
            jax.ShapeDtypeStruct((heads, n, LANES), BF16),
            jax.ShapeDtypeStruct((n, kv_rank), F32),
            jax.ShapeDtypeStruct((n, rope), F32),
            jax.ShapeDtypeStruct((n, kv_rank), BF16),
            jax.ShapeDtypeStruct((n, LANES), BF16),
            jax.ShapeDtypeStruct((n, conv_ch), F32),
            jax.ShapeDtypeStruct((n, pool_ch), F32),
        ],
        compiler_params=_cparams(("parallel",)),
        name="in_proj",
    )(x, win, wq, wuk, qg, kvg, cos_tab, sin_tab)


def _prompt_attn_kernel(qlat_ref, qpe_ref, kc_ref, kp_ref, o_ref, m_sc, l_sc, acc_sc,
                        *, heads, tile, kv_rank, batch):
    @pl.when(pl.program_id(0) >= batch)
    def _():
        o_ref[...] = jnp.zeros(o_ref.shape, o_ref.dtype)

    @pl.when(pl.program_id(0) < batch)
    def _():
        _prompt_attn_tile(qlat_ref, qpe_ref, kc_ref, kp_ref, o_ref, m_sc, l_sc, acc_sc,
                          heads=heads, tile=tile, kv_rank=kv_rank)


def _prompt_attn_tile(qlat_ref, qpe_ref, kc_ref, kp_ref, o_ref, m_sc, l_sc, acc_sc,
                      *, heads, tile, kv_rank):
    qi = pl.program_id(1)
    rows = heads * tile
    q = qlat_ref[...].reshape(rows, kv_rank)
    qp = qpe_ref[...].reshape(rows, LANES)
    m_sc[...] = jnp.full(m_sc.shape, -jnp.inf, F32)
    l_sc[...] = jnp.zeros(l_sc.shape, F32)
    acc_sc[...] = jnp.zeros(acc_sc.shape, F32)

    def step(ki, masked):
        start = pl.multiple_of(ki * tile, tile)
        kc = kc_ref[pl.ds(start, tile), :]
        kp = kp_ref[pl.ds(start, tile), :]
        s = _dot_nt(q, kc) + _dot_nt(qp, kp)
        if masked:
            r = lax.broadcasted_iota(jnp.int32, s.shape, 0) & (tile - 1)
            c = lax.broadcasted_iota(jnp.int32, s.shape, 1)
            s = jnp.where(c <= r, s, -jnp.inf)
        m_prev = m_sc[...]
        m_new = jnp.maximum(m_prev, jnp.max(s, axis=1, keepdims=True))
        p = jnp.exp(s - _lane_repeat(m_new, tile // LANES))
        alpha = jnp.exp(m_prev - m_new)
        l_sc[...] = alpha * l_sc[...] + jnp.sum(p, axis=1, keepdims=True)
        acc_sc[...] = acc_sc[...] * _lane_repeat(alpha, kv_rank // LANES) + _dot(p.astype(BF16), kc)
        m_sc[...] = m_new

    def body(ki, carry):
        step(ki, False)
        return carry

    lax.fori_loop(0, qi, body, 0)
    step(qi, True)
    inv = 1.0 / l_sc[...]
    o = acc_sc[...] * _lane_repeat(inv, kv_rank // LANES)
    o_ref[...] = o.reshape(heads, tile, kv_rank).astype(BF16)


def _prompt_attn(qlat, qpe, ckvb, kpeb, batch, seq, dm):
    heads, kv_rank = dm['heads'], dm['kv_rank']
    n = qlat.shape[1]
    tile = min(ATTN_TILE, seq)
    nq = seq // tile
    rows = heads * tile
    kern = functools.partial(_prompt_attn_kernel, heads=heads, tile=tile, kv_rank=kv_rank,
                             batch=batch)
    extra = (n - batch * seq) // tile
    assert (n - batch * seq) % tile == 0 and 1 <= extra <= nq
    qblk = lambda b, i: (0, jnp.where(b < batch, b * nq + i, batch * nq + jnp.minimum(i, extra - 1)), 0)
    kblk = lambda b, i: (jnp.minimum(b, batch - 1), 0)
    return pl.pallas_call(
        kern,
        grid=(batch + 1, nq),
        in_specs=[
            pl.BlockSpec((heads, tile, kv_rank), qblk),
            pl.BlockSpec((heads, tile, LANES), qblk),
            pl.BlockSpec((seq, kv_rank), kblk),
            pl.BlockSpec((seq, LANES), kblk),
        ],
        out_specs=pl.BlockSpec((heads, tile, kv_rank), qblk),
        out_shape=jax.ShapeDtypeStruct((heads, n, kv_rank), BF16),
        scratch_shapes=[
            pltpu.VMEM((rows, LANES), F32),
            pltpu.VMEM((rows, LANES), F32),
            pltpu.VMEM((rows, kv_rank), F32),
        ],
        compiler_params=_cparams(("parallel", "arbitrary")),
        name="prompt_attn",
    )(qlat, qpe, ckvb, kpeb)


def _decode_attn_kernel(pt_ref, qlat_ref, qpe_ref, kcn_ref, kpn_ref, cache_c, cache_p, buf_ref,
                        o_ref, kbuf, pbuf, sem,
                        *, layer, heads, group, dec_seq, n_pages, kv_rank, rope):
    del buf_ref
    g = pl.program_id(0)
    n_seq = pl.num_programs(0)
    slot = lax.rem(g, 2)
    j = lax.rem(g, group)
    rows_h = group * dec_seq
    rows = heads * rows_h

    def page_copies(seq, p, sl):
        pg = pt_ref[seq * n_pages + p]
        dst = pl.ds(pl.multiple_of(p * PAGE_SIZE, PAGE_SIZE), PAGE_SIZE)
        return (pltpu.make_async_copy(cache_c.at[layer, pg], kbuf.at[sl, dst], sem.at[0, sl]),
                pltpu.make_async_copy(cache_p.at[layer, pg], pbuf.at[sl, dst], sem.at[1, sl]))

    def start_seq(seq, sl):
        def body(p, c):
            for cp in page_copies(seq, p, sl):
                cp.start()
            return c
        lax.fori_loop(0, n_pages, body, 0)

    def wait_seq(seq, sl):
        def body(p, c):
            for cp in page_copies(seq, p, sl):
                cp.wait()
            return c
        lax.fori_loop(0, n_pages, body, 0)

    @pl.when(g == 0)
    def _():
        start_seq(g, slot)

    @pl.when(g + 1 < n_seq)
    def _():
        start_seq(g + 1, 1 - slot)

    wait_seq(g, slot)

    q = qlat_ref[...].reshape(rows, kv_rank)
    qp = qpe_ref[...].reshape(rows, LANES)
    kb = kbuf[slot].astype(BF16)
    pb = pbuf[slot].astype(BF16)
    s_past = _dot_nt(q, kb) + _dot_nt(qp[:, :rope], pb)

    kcn = kcn_ref[...]
    kpn = kpn_ref[...]
    s_new = _dot_nt(q, kcn) + _dot_nt(qp, kpn)
    r = lax.broadcasted_iota(jnp.int32, s_new.shape, 0)
    c = lax.broadcasted_iota(jnp.int32, s_new.shape, 1)
    shift = dec_seq.bit_length() - 1
    visible = ((c >> shift) == j) & ((c & (dec_seq - 1)) <= (r & (dec_seq - 1)))
    s_new = jnp.where(visible, s_new, -jnp.inf)

    m = jnp.maximum(jnp.max(s_past, axis=1, keepdims=True), jnp.max(s_new, axis=1, keepdims=True))
    p_past = jnp.exp(s_past - m)
    p_new = jnp.exp(s_new - m)
    denom = jnp.sum(p_past, axis=1, keepdims=True) + jnp.sum(p_new, axis=1, keepdims=True)
    o = (_dot(p_past.astype(BF16), kb) + _dot(p_new.astype(BF16), kcn)) / denom
    o = o.reshape(heads, rows_h, kv_rank).astype(BF16)

    @pl.when(j == 0)
    def _():
        o_ref[...] = o

    @pl.when(j != 0)
    def _():
        rr = lax.broadcasted_iota(jnp.int32, o.shape, 1)
        o_ref[...] = jnp.where((rr >> shift) == j, o, o_ref[...])


def _decode_attn(page_table, qlat, qpe, ckvb, kpeb, cache_ckv, cache_kpe, olat, layer,
                 n_prompt, dec_batch, dec_seq, dm):
    heads, kv_rank, rope = dm['heads'], dm['kv_rank'], dm['rope']
    group = DECODE_GROUP
    rows_h = group * dec_seq
    n_pages = page_table.shape[1]
    past = n_pages * PAGE_SIZE
    base = n_prompt // rows_h
    hrow = lambda g, pt: (0, base + g // group, 0)
    row = lambda g, pt: (base + g // group, 0)
    kern = functools.partial(_decode_attn_kernel, layer=layer, heads=heads, group=group,
                             dec_seq=dec_seq, n_pages=n_pages, kv_rank=kv_rank, rope=rope)
    grid_spec = pltpu.PrefetchScalarGridSpec(
        num_scalar_prefetch=1,
        grid=(dec_batch,),
        in_specs=[
            pl.BlockSpec((heads, rows_h, kv_rank), hrow),
            pl.BlockSpec((heads, rows_h, LANES), hrow),
            pl.BlockSpec((rows_h, kv_rank), row),
            pl.BlockSpec((rows_h, LANES), row),
            pl.BlockSpec(memory_space=pl.ANY),
            pl.BlockSpec(memory_space=pl.ANY),
            pl.BlockSpec(memory_space=pl.ANY),
        ],
        out_specs=pl.BlockSpec((heads, rows_h, kv_rank), hrow),
        scratch_shapes=[
            pltpu.VMEM((2, past, kv_rank), F32),
            pltpu.VMEM((2, past, rope), F32),
            pltpu.SemaphoreType.DMA((2, 2)),
        ],
    )
    return pl.pallas_call(
        kern,
        grid_spec=grid_spec,
        out_shape=jax.ShapeDtypeStruct(olat.shape, olat.dtype),
        input_output_aliases={7: 0},
        compiler_params=_cparams(("arbitrary",)),
        name="decode_attn",
    )(page_table.reshape(-1), qlat, qpe, ckvb, kpeb, cache_ckv, cache_kpe, olat)


def _pool_lane_select(shape, vals):
    lane_group = lax.broadcasted_iota(jnp.int32, shape, len(shape) - 1) // (shape[-1] // len(vals))
    out = vals[-1]
    for gi in range(len(vals) - 2, -1, -1):
        out = jnp.where(lane_group == gi, vals[gi], out)
    return out


def _seq_prompt_kernel(u_ref, hp_ref, cw_ref, cb_ref, cg_ref, cbeta_ref, pw_ref, ps_ref,
                       o_ref, cext, pext, *, batch, **kw):
    @pl.when(pl.program_id(0) >= batch)
    def _():
        o_ref[...] = jnp.zeros(o_ref.shape, o_ref.dtype)

    @pl.when(pl.program_id(0) < batch)
    def _():
        _seq_prompt_tile(u_ref, hp_ref, cw_ref, cb_ref, cg_ref, cbeta_ref, pw_ref, ps_ref,
                         o_ref, cext, pext, **kw)


def _seq_prompt_tile(u_ref, hp_ref, cw_ref, cb_ref, cg_ref, cbeta_ref, pw_ref, ps_ref,
                     o_ref, cext, pext, *, tile, conv_w, conv_ch, pool_ch):
    j = pl.program_id(1)
    chalo = cext.shape[0] - tile
    phalo = pext.shape[0] - tile

    @pl.when(j == 0)
    def _():
        cext[0:chalo, :] = jnp.zeros((chalo, conv_ch), F32)
        pext[0:phalo, :] = jnp.zeros((phalo, pool_ch), F32)

    cext[chalo:, :] = u_ref[...]
    pext[phalo:, :] = hp_ref[...]

    lead = chalo - (conv_w - 1)
    for c0 in range(0, tile, CONV_CHUNK):
        acc = None
        for k in range(conv_w):
            term = cext[pl.ds(c0 + lead + k, CONV_CHUNK), :] * cw_ref[k:k + 1, :]
            acc = term if acc is None else acc + term
        y = _layer_norm(acc + cb_ref[...], cg_ref[...], cbeta_ref[...])
        o_ref[c0:c0 + CONV_CHUNK, 0:conv_ch] = (y * _sigmoid(y)).astype(BF16)

    xe = pext[...]
    s2 = xe + pltpu.roll(xe, 1, 0)
    s4 = s2 + pltpu.roll(s2, 2, 0)
    s8 = s4 + pltpu.roll(s4, 4, 0)
    s16 = s8 + pltpu.roll(s8, 8, 0)
    sums = [s[phalo:, :] for s in (s2, s4, s8, s16)]
    v = xe[phalo:, :]
    shape = v.shape
    pos = j * tile + lax.broadcasted_iota(jnp.int32, shape, 0)
    win = _pool_lane_select(shape, [jnp.full(shape, w, jnp.int32) for w in POOL_WINDOWS])
    cnt = jnp.minimum(pos + 1, win).astype(F32)
    mean = _pool_lane_select(shape, sums) / cnt
    d = (mean - v).astype(BF16)
    o_ref[:, conv_ch:conv_ch + pool_ch] = (_dot(d, pw_ref[...]) * ps_ref[...]).astype(BF16)

    cext[0:chalo, :] = cext[tile:tile + chalo, :]
    pext[0:phalo, :] = pext[tile:tile + phalo, :]


def _seq_prompt(u, hp, cw, cb, cg, cbeta, pw, ps, batch, seq, dm):
    n = u.shape[0]
    conv_ch, pool_ch = dm['conv_ch'], dm['pool_ch']
    conv_w = cw.shape[0]
    tile = min(SEQ_TILE, seq)
    nt = seq // tile
    chalo = -(-(conv_w - 1) // 8) * 8
    phalo = -(-max(POOL_WINDOWS) // 8) * 8
    kern = functools.partial(_seq_prompt_kernel, batch=batch, tile=tile, conv_w=conv_w,
                             conv_ch=conv_ch, pool_ch=pool_ch)
    extra = (n - batch * seq) // tile
    assert (n - batch * seq) % tile == 0 and 1 <= extra <= nt
    row = lambda b, j: (jnp.where(b < batch, b * nt + j, batch * nt + jnp.minimum(j, extra - 1)), 0)
    return pl.pallas_call(
        kern,
        grid=(batch + 1, nt),
        in_specs=[
            pl.BlockSpec((tile, conv_ch), row),
            pl.BlockSpec((tile, pool_ch), row),
            _const_spec(cw.shape), _const_spec(cb.shape), _const_spec(cg.shape),
            _const_spec(cbeta.shape), _const_spec(pw.shape), _const_spec(ps.shape),
        ],
        out_specs=pl.BlockSpec((tile, conv_ch + pool_ch), row),
        out_shape=jax.ShapeDtypeStruct((n, conv_ch + pool_ch), BF16),
        scratch_shapes=[
            pltpu.VMEM((chalo + tile, conv_ch), F32),
            pltpu.VMEM((phalo + tile, pool_ch), F32),
        ],
        compiler_params=_cparams(("parallel", "arbitrary")),
        name="seq_prompt",
    )(u, hp, cw, cb, cg, cbeta, pw, ps)


def _seq_sample_kernel(cext_ref, pext_ref, cw_ref, cb_ref, cg_ref, cbeta_ref, pw_ref, ps_ref,
                       o_ref, *, dec_seq, conv_w, conv_ch, pool_ch, pos0):
    hist = pext_ref.shape[0] - dec_seq
    for t in range(dec_seq):
        acc = None
        for k in range(conv_w):
            term = cext_ref[t + k] * cw_ref[k:k + 1, :]
            acc = term if acc is None else acc + term
        y = _layer_norm(acc + cb_ref[...], cg_ref[...], cbeta_ref[...])
        o_ref[t, :, 0:conv_ch] = (y * _sigmoid(y)).astype(BF16)

        v = pext_ref[hist + t]
        sums = []
        run = None
        i = 0
        for w in POOL_WINDOWS:
            while i < w:
                row = pext_ref[hist + t - i]
                run = row if run is None else run + row
                i += 1
            sums.append(run / float(min(pos0 + t + 1, w)))
        mean = _pool_lane_select(v.shape, sums)
        d = (mean - v).astype(BF16)
        o_ref[t, :, conv_ch:conv_ch + pool_ch] = (_dot(d, pw_ref[...]) * ps_ref[...]).astype(BF16)


def _seq_sample(cext, pext, cw, cb, cg, cbeta, pw, ps, dec_seq, pos0, dm):
    conv_ch, pool_ch = dm['conv_ch'], dm['pool_ch']
    db = cext.shape[1]
    kern = functools.partial(_seq_sample_kernel, dec_seq=dec_seq, conv_w=cw.shape[0],
                             conv_ch=conv_ch, pool_ch=pool_ch, pos0=pos0)
    args = (cext, pext, cw, cb, cg, cbeta, pw, ps)
    return pl.pallas_call(
        kern,
        grid=(1,),
        in_specs=[_const_spec(a.shape) for a in args],
        out_specs=_const_spec((dec_seq, db, conv_ch + pool_ch)),
        out_shape=jax.ShapeDtypeStruct((dec_seq, db, conv_ch + pool_ch), BF16),
        compiler_params=_cparams(("arbitrary",)),
        name="seq_sample",
    )(*args)


def _out_proj_kernel(*refs, heads, att_width, alpha, route):
    if route:
        (olat_ref, ocp_ref, x_ref, wuv_ref, wout_ref, g_ref, b_ref, wr_ref, br_ref,
         x1_ref, x1b_ref, route_ref) = refs
    else:
        (olat_ref, ocp_ref, x_ref, wuv_ref, wout_ref, g_ref, b_ref, x1_ref, x1b_ref) = refs
    parts = [_dot(olat_ref[hd], wuv_ref[hd]) for hd in range(heads)]
    o_att = jnp.concatenate(parts, axis=1).astype(BF16)
    mix = _dot(o_att, wout_ref[0:att_width, :]) + _dot(ocp_ref[...], wout_ref[att_width:, :])
    x1 = _layer_norm(alpha * x_ref[...] + mix, g_ref[...], b_ref[...])
    x1_ref[...] = x1
    x1b_ref[...] = x1.astype(BF16)
    if route:
        logits = jnp.dot(x1, wr_ref[...], preferred_element_type=F32,
                         precision=lax.Precision.HIGHEST) + br_ref[...]
        lane = lax.broadcasted_iota(jnp.int32, logits.shape, 1)
        m1 = jnp.max(logits, axis=1, keepdims=True)
        i1 = jnp.min(jnp.where(logits == m1, lane, LANES), axis=1, keepdims=True)
        rest = jnp.where(lane == i1, -jnp.inf, logits)
        m2 = jnp.max(rest, axis=1, keepdims=True)
        i2 = jnp.min(jnp.where(rest == m2, lane, LANES), axis=1, keepdims=True)
        e2 = jnp.exp(m2 - m1)
        g1 = 1.0 / (1.0 + e2)
        g2 = e2 / (1.0 + e2)
        packed = jnp.where(lane == 0, i1.astype(F32),
                           jnp.where(lane == 1, i2.astype(F32),
                                     jnp.where(lane == 2, g1, jnp.where(lane == 3, g2, 0.0))))
        route_ref[...] = packed[:, :route_ref.shape[1]]


def _out_proj(olat, ocp, x, wuv, wout, g, b, router, dm):
    n, d = x.shape
    tm = TOKEN_TILE
    heads, kv_rank = dm['heads'], dm['kv_rank']
    row = lambda i: (i, 0)
    args = [olat, ocp, x, wuv, wout, g, b]
    in_specs = [
        pl.BlockSpec((heads, tm, kv_rank), lambda i: (0, i, 0)),
        pl.BlockSpec((tm, ocp.shape[1]), row),
        pl.BlockSpec((tm, d), row),
        _const_spec(wuv.shape), _const_spec(wout.shape), _const_spec(g.shape), _const_spec(b.shape),
    ]
    out_specs = [pl.BlockSpec((tm, d), row), pl.BlockSpec((tm, d), row)]
    out_shape = [jax.ShapeDtypeStruct((n, d), F32), jax.ShapeDtypeStruct((n, d), BF16)]
    if router is not None:
        wr, br = router
        args += [wr, br]
        in_specs += [_const_spec(wr.shape), _const_spec(br.shape)]
        out_specs.append(pl.BlockSpec((tm, 8), row))
        out_shape.append(jax.ShapeDtypeStruct((n, 8), F32))
    kern = functools.partial(_out_proj_kernel, heads=heads, att_width=heads * dm['v_dim'],
                             alpha=dm['alpha'], route=router is not None)
    return pl.pallas_call(
        kern, grid=(n // tm,), in_specs=in_specs, out_specs=out_specs, out_shape=out_shape,
        compiler_params=_cparams(("parallel",)), name="out_proj",
    )(*args)


def _ff_chunk(d_ff):
    best = LANES
    for c in range(LANES, 1024 + 1, LANES):
        if d_ff % c == 0:
            best = c
    return best


def _swiglu_tile(xb, wg_ref, wu_ref, wd_ref, d_ff, lead=()):
    chunk = _ff_chunk(d_ff)
    y = None
    for c0 in range(0, d_ff, chunk):
        gate = _dot(xb, wg_ref[lead + (slice(None), slice(c0, c0 + chunk))])
        up = _dot(xb, wu_ref[lead + (slice(None), slice(c0, c0 + chunk))])
        act = (gate * _sigmoid(gate) * up).astype(BF16)
        part = _dot(act, wd_ref[lead + (slice(c0, c0 + chunk), slice(None))])
        y = part if y is None else y + part
    return y


def _ffn_dense_kernel(xb_ref, x_ref, wg_ref, wu_ref, wd_ref, g_ref, b_ref, o_ref, *, d_ff, alpha):
    y = _swiglu_tile(xb_ref[...], wg_ref, wu_ref, wd_ref, d_ff)
    o_ref[...] = _layer_norm(alpha * x_ref[...] + y, g_ref[...], b_ref[...])


def _ffn_dense(x1b, x1, wg, wu, wd, g, b, dm):
    n, d = x1.shape
    tm = TOKEN_TILE
    row = lambda i: (i, 0)
    kern = functools.partial(_ffn_dense_kernel, d_ff=wg.shape[1], alpha=dm['alpha'])
    return pl.pallas_call(
        kern,
        grid=(n // tm,),
        in_specs=[pl.BlockSpec((tm, d), row), pl.BlockSpec((tm, d), row),
                  _resident_spec(wg.shape), _resident_spec(wu.shape), _resident_spec(wd.shape),
                  _const_spec(g.shape), _const_spec(b.shape)],
        out_specs=pl.BlockSpec((tm, d), row),
        out_shape=jax.ShapeDtypeStruct((n, d), F32),
        compiler_params=_cparams(("parallel",)),
        name="ffn_dense",
    )(x1b, x1, wg, wu, wd, g, b)


def _moe_experts_kernel(te_ref, nu_ref, xs_ref, wg_ref, wu_ref, wd_ref, o_ref, *, d_ff):
    del te_ref

    @pl.when(pl.program_id(0) < nu_ref[0])
    def _():
        o_ref[...] = _swiglu_tile(xs_ref[...], wg_ref, wu_ref, wd_ref, d_ff, lead=(0,))

    @pl.when(pl.program_id(0) >= nu_ref[0])
    def _():
        o_ref[...] = jnp.zeros(o_ref.shape, F32)


def _moe_experts(tile_expert, n_used, xs, wg, wu, wd):
    rows, d = xs.shape
    tm = TOKEN_TILE
    d_ff = wg.shape[2]
    wspec = lambda shape: pl.BlockSpec((1,) + shape[1:], lambda i, te, nu: (te[i], 0, 0),
                                       pipeline_mode=pl.Buffered(1))
    grid_spec = pltpu.PrefetchScalarGridSpec(
        num_scalar_prefetch=2,
        grid=(rows // tm,),
        in_specs=[pl.BlockSpec((tm, d), lambda i, te, nu: (i, 0)),
                  wspec(wg.shape), wspec(wu.shape), wspec(wd.shape)],
        out_specs=pl.BlockSpec((tm, d), lambda i, te, nu: (i, 0)),
    )
    return pl.pallas_call(
        functools.partial(_moe_experts_kernel, d_ff=d_ff),
        grid_spec=grid_spec,
        out_shape=jax.ShapeDtypeStruct((rows, d), F32),
        compiler_params=_cparams(("arbitrary",)),
        name="moe_experts",
    )(tile_expert, n_used, xs, wg, wu, wd)


def _moe_combine_kernel(x_ref, o1_ref, o2_ref, route_ref, g_ref, b_ref, o_ref, *, alpha):
    route = route_ref[...]
    y = route[:, 2:3] * o1_ref[...] + route[:, 3:4] * o2_ref[...]
    o_ref[...] = _layer_norm(alpha * x_ref[...] + y, g_ref[...], b_ref[...])


def _moe_combine(x1, o1, o2, route, g, b, dm):
    n, d = x1.shape
    tm = TOKEN_TILE
    row = lambda i: (i, 0)
    return pl.pallas_call(
        functools.partial(_moe_combine_kernel, alpha=dm['alpha']),
        grid=(n // tm,),
        in_specs=[pl.BlockSpec((tm, d), row), pl.BlockSpec((tm, d), row), pl.BlockSpec((tm, d), row),
                  pl.BlockSpec((tm, route.shape[1]), row), _const_spec(g.shape), _const_spec(b.shape)],
        out_specs=pl.BlockSpec((tm, d), row),
        out_shape=jax.ShapeDtypeStruct((n, d), F32),
        compiler_params=_cparams(("parallel",)),
        name="moe_combine",
    )(x1, o1, o2, route, g, b)


def _moe_layer(x1, x1b, route, wg, wu, wd, g, b, dm):
    n = x1.shape[0]
    n_exp = wg.shape[0]
    tm = TOKEN_TILE
    n_assign = TOP_K * n
    n_tiles = n_assign // tm + n_exp
    e_flat = jnp.concatenate([route[:, 0], route[:, 1]]).astype(jnp.int32)
    onehot = (e_flat[:, None] == jnp.arange(n_exp, dtype=jnp.int32)[None, :]).astype(jnp.int32)
    cnt = jnp.sum(onehot, axis=0)
    rank = jnp.sum((jnp.cumsum(onehot, axis=0) - onehot) * onehot, axis=1)
    padded = ((cnt + tm - 1) // tm) * tm
    pad_end = jnp.cumsum(padded)
    pad_start = pad_end - padded
    cnt_start = jnp.cumsum(cnt) - cnt
    pos = pad_start[e_flat] + rank
    order = jnp.argsort(e_flat, stable=True).astype(jnp.int32)
    slot = jnp.arange(n_tiles * tm, dtype=jnp.int32)
    tile_expert = jnp.minimum(
        jnp.searchsorted(pad_end, jnp.arange(n_tiles, dtype=jnp.int32) * tm, side='right'),
        n_exp - 1).astype(jnp.int32)
    slot_e = jnp.repeat(tile_expert, tm)
    local = slot - pad_start[slot_e]
    src = jnp.where(local < cnt[slot_e], order[jnp.minimum(cnt_start[slot_e] + local, n_assign - 1)], 0)
    tok_sorted = jnp.where(src >= n, src - n, src)
    n_used = (pad_end[-1:] // tm).astype(jnp.int32)

    xs = jnp.take(x1b, tok_sorted, axis=0)
    out_sorted = _moe_experts(tile_expert, n_used, xs, wg, wu, wd)
    o1 = jnp.take(out_sorted, pos[:n], axis=0)
    o2 = jnp.take(out_sorted, pos[n:], axis=0)
    return _moe_combine(x1, o1, o2, route, g, b, dm)


def _prep_in_weights(w_in, dm):
    q_rank, kv_rank, rope, conv_ch, pool_ch = (dm['q_rank'], dm['kv_rank'], dm['rope'],
                                               dm['conv_ch'], dm['pool_ch'])
    a = q_rank + kv_rank
    w_q, w_kv = w_in[..., :q_rank], w_in[..., q_rank:a]
    w_pe = w_in[..., a:a + rope]
    w_glu = w_in[..., a + rope:a + rope + 2 * conv_ch]
    w_pool = w_in[..., a + rope + 2 * conv_ch:]
    half = rope // 2
    w_sw = jnp.concatenate([-w_pe[..., half:], w_pe[..., :half]], axis=-1)
    zpad = jnp.zeros(w_pe.shape[:-1] + (LANES - rope,), w_in.dtype)
    return jnp.concatenate([w_q, w_kv, w_glu, w_pool, w_pe, zpad, w_sw, zpad], axis=-1).astype(BF16)


def _prep_q_weights(w_q_b, dm):
    depth, q_rank, heads, _ = w_q_b.shape
    nope, rope = dm['nope'], dm['rope']
    half = rope // 2
    w_nope = w_q_b[..., :nope].reshape(depth, q_rank, heads * nope)
    w_pe = w_q_b[..., nope:]
    w_sw = jnp.concatenate([-w_pe[..., half:], w_pe[..., :half]], axis=-1)
    zpad = jnp.zeros(w_pe.shape[:-1] + (LANES - rope,), w_q_b.dtype)
    pad = lambda w: jnp.concatenate([w, zpad], axis=-1).reshape(depth, q_rank, heads * LANES)
    return jnp.concatenate([w_nope, pad(w_pe), pad(w_sw)], axis=-1).astype(BF16)


def _rope_tables(seq, dec_seq, dec_tokens, past, rope):
    half = rope // 2
    inv = ROPE_BASE ** (-jnp.arange(half, dtype=F32) / half)

    def tab(pos):
        ang = pos.astype(F32)[:, None] * inv[None, :]
        z = jnp.zeros((pos.shape[0], LANES - rope), F32)
        c, s = jnp.cos(ang), jnp.sin(ang)
        return jnp.concatenate([c, c, z], axis=1), jnp.concatenate([s, s, z], axis=1)

    cp, sp = tab(jnp.arange(seq, dtype=jnp.int32))
    cs, ss = tab(past + jnp.arange(dec_seq, dtype=jnp.int32))
    reps = dec_tokens // dec_seq
    return (jnp.concatenate([cp, jnp.tile(cs, (reps, 1))], axis=0),
            jnp.concatenate([sp, jnp.tile(ss, (reps, 1))], axis=0))


def kernel(x_prompt, x_sample, cache_ckv, cache_kpe, state_conv, state_pool, page_table, w_in, q_norm_g, kv_norm_g, w_q_b, w_uk, w_uv, conv_w, conv_b, conv_ln_g, conv_ln_b, pool_w, pool_scale, w_out, ln1_g, ln1_b, w_ffn_gate, w_ffn_up, w_ffn_down, w_router, b_router, w_exp_gate, w_exp_up, w_exp_down, ln2_g, ln2_b):
    batch, seq, d_model = x_prompt.shape
    dec_batch, dec_seq, _ = x_sample.shape
    depth = w_in.shape[0]
    q_rank, heads, qk = w_q_b.shape[1:]
    kv_rank, _, nope = w_uk.shape[1:]
    v_dim = w_uv.shape[3]
    rope = qk - nope
    conv_ch = conv_w.shape[2]
    pool_ch = pool_scale.shape[1]
    conv_past = conv_w.shape[1] - 1
    pool_past = state_pool.shape[2]
    n_groups, pool_gc = pool_w.shape[1], pool_w.shape[2]
    n_exp = w_router.shape[2]
    past = page_table.shape[1] * PAGE_SIZE
    n_p, n_s = batch * seq, dec_batch * dec_seq
    tm = TOKEN_TILE
    assert n_p % tm == 0 and n_s % tm == 0 and seq % tm == 0
    assert cache_ckv.shape[2] == PAGE_SIZE and n_groups == len(POOL_WINDOWS)
    assert dec_batch % DECODE_GROUP == 0 and n_p % (DECODE_GROUP * dec_seq) == 0
    assert dec_seq & (dec_seq - 1) == 0 and ATTN_TILE & (ATTN_TILE - 1) == 0
    dm = dict(heads=heads, q_rank=q_rank, kv_rank=kv_rank, nope=nope, rope=rope, v_dim=v_dim,
              conv_ch=conv_ch, pool_ch=pool_ch, att_scale=float(qk) ** -0.5,
              alpha=float(2 * depth) ** 0.25)

    win_all = _prep_in_weights(w_in, dm)
    wq_all = _prep_q_weights(w_q_b, dm)
    wuk_all = jnp.transpose(w_uk, (0, 2, 3, 1)).astype(BF16)
    wuv_all = jnp.transpose(w_uv, (0, 2, 1, 3)).astype(BF16)
    eye = jnp.eye(n_groups, dtype=pool_w.dtype)
    pw_all = jnp.einsum('lgcd,gh->lgchd', pool_w, eye).reshape(depth, pool_ch, pool_ch).astype(BF16)
    wout_all = w_out.astype(BF16)
    wfg, wfu, wfd = w_ffn_gate.astype(BF16), w_ffn_up.astype(BF16), w_ffn_down.astype(BF16)
    weg, weu, wed = w_exp_gate.astype(BF16), w_exp_up.astype(BF16), w_exp_down.astype(BF16)
    wr_all = jnp.pad(w_router, ((0, 0), (0, 0), (0, LANES - n_exp)))
    br_all = jnp.pad(b_router, ((0, 0), (0, LANES - n_exp)), constant_values=NEG_BIG)[:, None, :]
    vec = lambda a, l: a[l][None, :]

    cos_tab, sin_tab = _rope_tables(seq, dec_seq, n_s, past, rope)
    tiles_per_seq, prompt_tiles = seq // tm, n_p // tm
    tab_index = lambda i: jnp.where(i < prompt_tiles, lax.rem(i, tiles_per_seq),
                                    tiles_per_seq + i - prompt_tiles)

    x = jnp.concatenate([x_prompt.reshape(n_p, d_model), x_sample.reshape(n_s, d_model)], axis=0)
    outs = [[] for _ in range(8)]
    for l in range(depth):
        qlat, qpe, ckv, kpe, ckvb, kpeb, u, hp = _in_proj(
            x, win_all[l], wq_all[l], wuk_all[l], vec(q_norm_g, l), vec(kv_norm_g, l),
            cos_tab, sin_tab, tab_index, dm)

        olat = _prompt_attn(qlat, qpe, ckvb, kpeb, batch, seq, dm)
        olat = _decode_attn(page_table, qlat, qpe, ckvb, kpeb, cache_ckv, cache_kpe, olat, l,
                            n_p, dec_batch, dec_seq, dm)

        seq_w = (conv_w[l], vec(conv_b, l), vec(conv_ln_g, l), vec(conv_ln_b, l),
                 pw_all[l], vec(pool_scale, l))
        ocp = _seq_prompt(u, hp, *seq_w, batch, seq, dm)
        cext = jnp.concatenate([state_conv[l], u[n_p:].reshape(dec_batch, dec_seq, conv_ch)], axis=1)
        pext = jnp.concatenate([state_pool[l], hp[n_p:].reshape(dec_batch, dec_seq, pool_ch)], axis=1)
        ocp_s = _seq_sample(jnp.swapaxes(cext, 0, 1), jnp.swapaxes(pext, 0, 1), *seq_w,
                            dec_seq, past, dm)
        ocp = lax.dynamic_update_slice(
            ocp, jnp.swapaxes(ocp_s, 0, 1).reshape(n_s, conv_ch + pool_ch), (n_p, 0))

        i = l // 2
        router = (wr_all[i], br_all[i]) if l % 2 == 1 else None
        res = _out_proj(olat, ocp, x, wuv_all[l], wout_all[l], vec(ln1_g, l), vec(ln1_b, l),
                        router, dm)
        if l % 2 == 0:
            x1, x1b = res
            x = _ffn_dense(x1b, x1, wfg[i], wfu[i], wfd[i], vec(ln2_g, l), vec(ln2_b, l), dm)
        else:
            x1, x1b, route = res
            x = _moe_layer(x1, x1b, route, weg[i], weu[i], wed[i], vec(ln2_g, l), vec(ln2_b, l), dm)

        outs[0].append(ckv[:n_p].reshape(batch, seq, kv_rank))
        outs[1].append(kpe[:n_p].reshape(batch, seq, rope))
        outs[2].append(u[:n_p].reshape(batch, seq, conv_ch)[:, seq - conv_past:])
        outs[3].append(hp[:n_p].reshape(batch, seq, pool_ch)[:, seq - pool_past:])
        outs[4].append(ckv[n_p:].reshape(dec_batch, dec_seq, kv_rank))
        outs[5].append(kpe[n_p:].reshape(dec_batch, dec_seq, rope))
        outs[6].append(cext[:, dec_seq:])
        outs[7].append(pext[:, dec_seq:])

    return (x[:n_p].reshape(batch, seq, d_model), x[n_p:].reshape(dec_batch, dec_seq, d_model),
            *[jnp.stack(o) for o in outs])
```

```python
import functools

import jax
import jax.numpy as jnp
from jax import lax
from jax.experimental import pallas as pl
from jax.experimental.pallas import tpu as pltpu

F32 = jnp.float32
BF16 = jnp.bfloat16

LN_EPS = 1e-5
RMS_EPS = 1e-6
ROPE_BASE = 10000.0
PAGE_SIZE = 128
TOP_K = 2
POOL_WINDOWS = (2, 4, 8, 16)

LANES = 128
SUBLANES = 8
V7X_VMEM_LIMIT = 56 * 1024 * 1024
TOKEN_TILE = 512
ATTN_TILE = 512
SEQ_TILE = 256
CONV_CHUNK = 64
DECODE_GROUP = 4
MXU_TILE = 256
FF_CHUNK = 4 * MXU_TILE
NEG_BIG = -1e30
LOG2_E = 1.4426950408889634


def _cparams(sem):
    return pltpu.CompilerParams(dimension_semantics=sem, vmem_limit_bytes=V7X_VMEM_LIMIT)


def _dot(a, b):
    return jnp.dot(a, b, preferred_element_type=F32)


def _dot_nt(a, b):
    return lax.dot_general(a, b, (((1,), (1,)), ((), ())), preferred_element_type=F32)


def _lane_repeat(x, n):
    return x if n == 1 else jnp.concatenate([x] * n, axis=1)


def _sigmoid(x):
    return 1.0 / (1.0 + jnp.exp(-x))


def _layer_norm(y, g, b):
    mu = jnp.mean(y, axis=-1, keepdims=True)
    yc = y - mu
    var = jnp.mean(yc * yc, axis=-1, keepdims=True)
    return yc * lax.rsqrt(var + LN_EPS) * g + b


def _rms_norm(h, g):
    return h * lax.rsqrt(jnp.mean(h * h, axis=-1, keepdims=True) + RMS_EPS) * g


def _const_spec(shape):
    nd = len(shape)
    return pl.BlockSpec(shape, lambda *_: (0,) * nd)


def _resident_spec(shape):
    nd = len(shape)
    return pl.BlockSpec(shape, lambda *_: (0,) * nd, pipeline_mode=pl.Buffered(1))


def _in_proj_kernel(x_ref, win_ref, wq_ref, wuk_ref, qg_ref, kvg_ref, cos_ref, sin_ref,
                    qcat_ref, kcat_ref, ckvp_ref, kpep_ref, ckvs_ref, kpes_ref, u_ref, hp_ref,
                    *, heads, q_rank, kv_rank, conv_ch, pool_ch, nope, rope, scale, prompt_tiles):
    x = x_ref[...].astype(BF16)
    h = _dot(x, win_ref[...])
    o_kv = q_rank
    o_glu = o_kv + kv_rank
    o_pool = o_glu + 2 * conv_ch
    o_pe = o_pool + pool_ch
    o_sw = o_pe + LANES
    cos = cos_ref[...]
    sin = sin_ref[...]

    qn = _rms_norm(h[:, :q_rank], qg_ref[...]).astype(BF16)
    q = _dot(qn, wq_ref[...])
    o_qpe = heads * nope
    o_qsw = o_qpe + heads * LANES
    for hd in range(heads):
        q_nope = q[:, hd * nope:(hd + 1) * nope].astype(BF16)
        qcat_ref[hd, :, 0:kv_rank] = (_dot(q_nope, wuk_ref[hd]) * scale).astype(BF16)
        pe = q[:, o_qpe + hd * LANES:o_qpe + (hd + 1) * LANES]
        sw = q[:, o_qsw + hd * LANES:o_qsw + (hd + 1) * LANES]
        qcat_ref[hd, :, kv_rank:] = ((pe * cos + sw * sin) * scale).astype(BF16)

    ckv = _rms_norm(h[:, o_kv:o_kv + kv_rank], kvg_ref[...])
    kpe = h[:, o_pe:o_pe + LANES] * cos + h[:, o_sw:o_sw + LANES] * sin
    kcat_ref[:, 0:kv_rank] = ckv.astype(BF16)
    kcat_ref[:, kv_rank:] = kpe.astype(BF16)

    @pl.when(pl.program_id(0) < prompt_tiles)
    def _():
        ckvp_ref[...] = ckv
        kpep_ref[...] = kpe[:, :rope]

    @pl.when(pl.program_id(0) >= prompt_tiles)
    def _():
        ckvs_ref[...] = ckv
        kpes_ref[...] = kpe[:, :rope]

    u_ref[...] = h[:, o_glu:o_glu + conv_ch] * _sigmoid(h[:, o_glu + conv_ch:o_glu + 2 * conv_ch])
    hp_ref[...] = h[:, o_pool:o_pool + pool_ch]


def _in_proj(x, win, wq, wuk, qg, kvg, cos_tab, sin_tab, tab_index, n_prompt, dm):
    n, d = x.shape
    tm = TOKEN_TILE
    heads, q_rank, kv_rank = dm['heads'], dm['q_rank'], dm['kv_rank']
    conv_ch, pool_ch, rope = dm['conv_ch'], dm['pool_ch'], dm['rope']
    prompt_tiles = n_prompt // tm
    row = lambda i: (i, 0)
    hrow = lambda i: (0, i, 0)
    prow = lambda i: (jnp.minimum(i, prompt_tiles - 1), 0)
    srow = lambda i: (jnp.maximum(i - prompt_tiles, 0), 0)
    kern = functools.partial(_in_proj_kernel, heads=heads, q_rank=q_rank, kv_rank=kv_rank,
                             conv_ch=conv_ch, pool_ch=pool_ch, nope=dm['nope'], rope=rope,
                             scale=dm['q_scale'], prompt_tiles=prompt_tiles)
    return pl.pallas_call(
        kern,
        grid=(n // tm,),
        in_specs=[
            pl.BlockSpec((tm, d), row),
            _const_spec(win.shape), _const_spec(wq.shape), _const_spec(wuk.shape),
            _const_spec(qg.shape), _const_spec(kvg.shape),
            pl.BlockSpec((tm, LANES), lambda i: (tab_index(i), 0)),
            pl.BlockSpec((tm, LANES), lambda i: (tab_index(i), 0)),
        ],
        out_specs=[
            pl.BlockSpec((heads, tm, kv_rank + LANES), hrow),
            pl.BlockSpec((tm, kv_rank + LANES), row),
            pl.BlockSpec((tm, kv_rank), prow),
            pl.BlockSpec((tm, rope), prow),
            pl.BlockSpec((tm, kv_rank), srow),
            pl.BlockSpec((tm, rope), srow),
            pl.BlockSpec((tm, conv_ch), row),
            pl.BlockSpec((tm, pool_ch), row),
        ],
        out_shape=[
            jax.ShapeDtypeStruct((heads, n, kv_rank + LANES), BF16),
            jax.ShapeDtypeStruct((n, kv_rank + LANES), BF16),
            jax.ShapeDtypeStruct((n_prompt, kv_rank), F32),
            jax.ShapeDtypeStruct((n_prompt, rope), F32),
            jax.ShapeDtypeStruct((n - n_prompt, kv_rank), F32),
            jax.ShapeDtypeStruct((n - n_prompt, rope), F32),
            jax.ShapeDtypeStruct((n, conv_ch), F32),
            jax.ShapeDtypeStruct((n, pool_ch), F32),
        ],
        compiler_params=_cparams(("arbitrary",)),
        name="in_proj",
    )(x, win, wq, wuk, qg, kvg, cos_tab, sin_tab)


def _prompt_attn_kernel(q_ref, k_ref, o_ref, m_sc, l_sc, acc_sc, *, heads, tile, kv_rank, batch):
    @pl.when(pl.program_id(0) >= batch)
    def _():
        o_ref[...] = jnp.zeros(o_ref.shape, o_ref.dtype)

    @pl.when(pl.program_id(0) < batch)
    def _():
        _prompt_attn_tile(q_ref, k_ref, o_ref, m_sc, l_sc, acc_sc,
                          heads=heads, tile=tile, kv_rank=kv_rank)


def _prompt_attn_tile(q_ref, k_ref, o_ref, m_sc, l_sc, acc_sc, *, heads, tile, kv_rank):
    qi = pl.program_id(1)
    rows = heads * tile
    q = q_ref[...].reshape(rows, q_ref.shape[2])
    m_sc[...] = jnp.full(m_sc.shape, -jnp.inf, F32)
    l_sc[...] = jnp.zeros(l_sc.shape, F32)
    acc_sc[...] = jnp.zeros(acc_sc.shape, F32)

    def step(ki, masked):
        start = pl.multiple_of(ki * tile, tile)
        k = k_ref[pl.ds(start, tile), :]
        s = _dot_nt(q, k)
        if masked:
            r = lax.broadcasted_iota(jnp.int32, s.shape, 0) & (tile - 1)
            c = lax.broadcasted_iota(jnp.int32, s.shape, 1)
            s = jnp.where(c <= r, s, -jnp.inf)
        m_prev = m_sc[...]
        m_new = jnp.maximum(m_prev, jnp.max(s, axis=1, keepdims=True))
        p = jnp.exp2(s - _lane_repeat(m_new, tile // LANES))
        alpha = jnp.exp2(m_prev - m_new)
        l_sc[...] = alpha * l_sc[...] + jnp.sum(p, axis=1, keepdims=True)
        acc_sc[...] = (acc_sc[...] * _lane_repeat(alpha, kv_rank // LANES)
                       + _dot(p.astype(BF16), k[:, :kv_rank]))
        m_sc[...] = m_new

    def body(ki, carry):
        step(ki, False)
        return carry

    lax.fori_loop(0, qi, body, 0)
    step(qi, True)
    inv = 1.0 / l_sc[...]
    o = acc_sc[...] * _lane_repeat(inv, kv_rank // LANES)
    o_ref[...] = o.reshape(heads, tile, kv_rank).astype(BF16)


def _prompt_attn(qcat, kcat, batch, seq, dm):
    heads, kv_rank = dm['heads'], dm['kv_rank']
    n, width = qcat.shape[1:]
    tile = min(ATTN_TILE, seq)
    nq = seq // tile
    rows = heads * tile
    kern = functools.partial(_prompt_attn_kernel, heads=heads, tile=tile, kv_rank=kv_rank,
                             batch=batch)
    extra = (n - batch * seq) // tile
    assert (n - batch * seq) % tile == 0 and 1 <= extra <= nq
    qblk = lambda b, i: (0, jnp.where(b < batch, b * nq + i, batch * nq + jnp.minimum(i, extra - 1)), 0)
    kblk = lambda b, i: (jnp.minimum(b, batch - 1), 0)
    return pl.pallas_call(
        kern,
        grid=(batch + 1, nq),
        in_specs=[
            pl.BlockSpec((heads, tile, width), qblk),
            pl.BlockSpec((seq, width), kblk),
        ],
        out_specs=pl.BlockSpec((heads, tile, kv_rank), qblk),
        out_shape=jax.ShapeDtypeStruct((heads, n, kv_rank), BF16),
        scratch_shapes=[
            pltpu.VMEM((rows, LANES), F32),
            pltpu.VMEM((rows, LANES), F32),
            pltpu.VMEM((rows, kv_rank), F32),
        ],
        compiler_params=_cparams(("parallel", "arbitrary")),
        name="prompt_attn",
    )(qcat, kcat)


def _decode_attn_kernel(pt_ref, q_ref, kn_ref, cache_c, cache_p, buf_ref,
                        o_ref, kbuf, pbuf, sem,
                        *, layer, heads, group, dec_seq, n_pages, kv_rank, rope):
    del buf_ref
    g = pl.program_id(0)
    n_seq = pl.num_programs(0)
    slot = lax.rem(g, 2)
    j = lax.rem(g, group)
    rows_h = group * dec_seq
    rows = heads * rows_h

    def start_seq(seq, sl):
        def body(p, c):
            pg = pt_ref[seq * n_pages + p]
            dst = pl.ds(pl.multiple_of(p * PAGE_SIZE, PAGE_SIZE), PAGE_SIZE)
            pltpu.make_async_copy(cache_c.at[layer, pg], kbuf.at[sl, dst], sem.at[0, sl]).start()
            pltpu.make_async_copy(cache_p.at[layer, pg], pbuf.at[sl, :, dst], sem.at[1, sl]).start()
            return c
        lax.fori_loop(0, n_pages, body, 0, unroll=min(8, n_pages))

    def wait_seq(sl):
        pltpu.make_async_copy(kbuf.at[sl], kbuf.at[sl], sem.at[0, sl]).wait()
        pltpu.make_async_copy(pbuf.at[sl], pbuf.at[sl], sem.at[1, sl]).wait()

    @pl.when(g == 0)
    def _():
        start_seq(g, slot)

    @pl.when(g + 1 < n_seq)
    def _():
        start_seq(g + 1, 1 - slot)

    wait_seq(slot)

    q = q_ref[...].reshape(rows, q_ref.shape[2])
    kb = kbuf[slot].astype(BF16)
    pb = pbuf[slot].astype(BF16)
    s_past = _dot_nt(q[:, :kv_rank], kb) + _dot(q[:, kv_rank:kv_rank + rope], pb)

    kn = kn_ref[...]
    s_new = _dot_nt(q, kn)
    r = lax.broadcasted_iota(jnp.int32, s_new.shape, 0)
    c = lax.broadcasted_iota(jnp.int32, s_new.shape, 1)
    shift = dec_seq.bit_length() - 1
    visible = ((c >> shift) == j) & ((c & (dec_seq - 1)) <= (r & (dec_seq - 1)))
    s_new = jnp.where(visible, s_new, -jnp.inf)

    m = jnp.maximum(jnp.max(s_past, axis=1, keepdims=True), jnp.max(s_new, axis=1, keepdims=True))
    p_past = jnp.exp2(s_past - m)
    p_new = jnp.exp2(s_new - m)
    denom = jnp.sum(p_past, axis=1, keepdims=True) + jnp.sum(p_new, axis=1, keepdims=True)
    o = (_dot(p_past.astype(BF16), kb) + _dot(p_new.astype(BF16), kn[:, :kv_rank])) / denom
    o = o.reshape(heads, rows_h, kv_rank).astype(BF16)

    @pl.when(j == 0)
    def _():
        o_ref[...] = o

    @pl.when(j != 0)
    def _():
        rr = lax.broadcasted_iota(jnp.int32, o.shape, 1)
        o_ref[...] = jnp.where((rr >> shift) == j, o, o_ref[...])


def _decode_attn(page_table, qcat, kcat, cache_ckv, cache_kpe_t, olat, layer,
                 n_prompt, dec_batch, dec_seq, dm):
    heads, kv_rank, rope = dm['heads'], dm['kv_rank'], dm['rope']
    width = qcat.shape[2]
    group = DECODE_GROUP
    rows_h = group * dec_seq
    n_pages = page_table.shape[1]
    past = n_pages * PAGE_SIZE
    base = n_prompt // rows_h
    hrow = lambda g, pt: (0, base + g // group, 0)
    row = lambda g, pt: (base + g // group, 0)
    kern = functools.partial(_decode_attn_kernel, layer=layer, heads=heads, group=group,
                             dec_seq=dec_seq, n_pages=n_pages, kv_rank=kv_rank, rope=rope)
    grid_spec = pltpu.PrefetchScalarGridSpec(
        num_scalar_prefetch=1,
        grid=(dec_batch,),
        in_specs=[
            pl.BlockSpec((heads, rows_h, width), hrow),
            pl.BlockSpec((rows_h, width), row),
            pl.BlockSpec(memory_space=pl.ANY),
            pl.BlockSpec(memory_space=pl.ANY),
            pl.BlockSpec(memory_space=pl.ANY),
        ],
        out_specs=pl.BlockSpec((heads, rows_h, kv_rank), hrow),
        scratch_shapes=[
            pltpu.VMEM((2, past, kv_rank), F32),
            pltpu.VMEM((2, rope, past), F32),
            pltpu.SemaphoreType.DMA((2, 2)),
        ],
    )
    return pl.pallas_call(
        kern,
        grid_spec=grid_spec,
        out_shape=jax.ShapeDtypeStruct(olat.shape, olat.dtype),
        input_output_aliases={5: 0},
        compiler_params=_cparams(("arbitrary",)),
        name="decode_attn",
    )(page_table.reshape(-1), qcat, kcat, cache_ckv, cache_kpe_t, olat)


def _pool_lane_select(shape, vals):
    lane_group = lax.broadcasted_iota(jnp.int32, shape, len(shape) - 1) // (shape[-1] // len(vals))
    out = vals[-1]
    for gi in range(len(vals) - 2, -1, -1):
        out = jnp.where(lane_group == gi, vals[gi], out)
    return out


def _seq_prompt_kernel(u_ref, hp_ref, cw_ref, cb_ref, cg_ref, cbeta_ref, pw_ref, ps_ref,
                       o_ref, cext, pext, shifted, *, batch, **kw):
    @pl.when(pl.program_id(0) >= batch)
    def _():
        o_ref[...] = jnp.zeros(o_ref.shape, o_ref.dtype)

    @pl.when(pl.program_id(0) < batch)
    def _():
        _seq_prompt_tile(u_ref, hp_ref, cw_ref, cb_ref, cg_ref, cbeta_ref, pw_ref, ps_ref,
                         o_ref, cext, pext, shifted, **kw)


def _seq_prompt_tile(u_ref, hp_ref, cw8_ref, cb_ref, cg_ref, cbeta_ref, pw_ref, ps_ref,
                     o_ref, cext, pext, shifted, *, tile, conv_w, conv_ch, pool_ch):
    j = pl.program_id(1)
    chalo = cext.shape[0] - tile
    phalo = pext.shape[0] - tile

    @pl.when(j == 0)
    def _():
        cext[0:chalo, :] = jnp.zeros((chalo, conv_ch), F32)
        pext[0:phalo, :] = jnp.zeros((phalo, pool_ch), F32)

    cext[chalo:, :] = u_ref[...]
    pext[phalo:, :] = hp_ref[...]

    span = cext.shape[0] - SUBLANES
    for b in range(1, SUBLANES):
        shifted[b, 0:span, :] = cext[pl.ds(b, span), :]
    lead = chalo - (conv_w - 1)
    for c0 in range(0, tile, CONV_CHUNK):
        acc = None
        for k in range(conv_w):
            a, b = divmod(lead + k, SUBLANES)
            win = pl.ds(c0 + a * SUBLANES, CONV_CHUNK)
            rows = cext[win, :] if b == 0 else shifted[b, win, :]
            term = rows.reshape(CONV_CHUNK // SUBLANES, SUBLANES, conv_ch) * cw8_ref[k][None]
            acc = term if acc is None else acc + term
        acc = acc.reshape(CONV_CHUNK, conv_ch)
        y = _layer_norm(acc + cb_ref[...], cg_ref[...], cbeta_ref[...])
        o_ref[c0:c0 + CONV_CHUNK, 0:conv_ch] = (y * _sigmoid(y)).astype(BF16)

    xe = pext[...]
    s2 = xe + pltpu.roll(xe, 1, 0)
    s4 = s2 + pltpu.roll(s2, 2, 0)
    s8 = s4 + pltpu.roll(s4, 4, 0)
    s16 = s8 + pltpu.roll(s8, 8, 0)
    sums = [s[phalo:, :] for s in (s2, s4, s8, s16)]
    v = xe[phalo:, :]
    shape = v.shape
    pos = j * tile + lax.broadcasted_iota(jnp.int32, shape, 0)
    win = _pool_lane_select(shape, [jnp.full(shape, w, jnp.int32) for w in POOL_WINDOWS])
    cnt = jnp.minimum(pos + 1, win).astype(F32)
    mean = _pool_lane_select(shape, sums) / cnt
    d = (mean - v).astype(BF16)
    o_ref[:, conv_ch:conv_ch + pool_ch] = (_dot(d, pw_ref[...]) * ps_ref[...]).astype(BF16)

    cext[0:chalo, :] = cext[tile:tile + chalo, :]
    pext[0:phalo, :] = pext[tile:tile + phalo, :]


def _seq_prompt(u, hp, cw, cb, cg, cbeta, pw, ps, batch, seq, dm):
    n = u.shape[0]
    conv_ch, pool_ch = dm['conv_ch'], dm['pool_ch']
    conv_w = cw.shape[0]
    cw = jnp.broadcast_to(cw[:, None, :], (conv_w, SUBLANES, conv_ch))
    tile = min(SEQ_TILE, seq)
    nt = seq // tile
    chalo = -(-(conv_w - 1) // 8) * 8
    phalo = -(-max(POOL_WINDOWS) // 8) * 8
    kern = functools.partial(_seq_prompt_kernel, batch=batch, tile=tile, conv_w=conv_w,
                             conv_ch=conv_ch, pool_ch=pool_ch)
    extra = (n - batch * seq) // tile
    assert (n - batch * seq) % tile == 0 and 1 <= extra <= nt
    row = lambda b, j: (jnp.where(b < batch, b * nt + j, batch * nt + jnp.minimum(j, extra - 1)), 0)
    return pl.pallas_call(
        kern,
        grid=(batch + 1, nt),
        in_specs=[
            pl.BlockSpec((tile, conv_ch), row),
            pl.BlockSpec((tile, pool_ch), row),
            _const_spec(cw.shape), _const_spec(cb.shape), _const_spec(cg.shape),
            _const_spec(cbeta.shape), _const_spec(pw.shape), _const_spec(ps.shape),
        ],
        out_specs=pl.BlockSpec((tile, conv_ch + pool_ch), row),
        out_shape=jax.ShapeDtypeStruct((n, conv_ch + pool_ch), BF16),
        scratch_shapes=[
            pltpu.VMEM((chalo + tile, conv_ch), F32),
            pltpu.VMEM((phalo + tile, pool_ch), F32),
            pltpu.VMEM((SUBLANES, chalo + tile - SUBLANES, conv_ch), F32),
        ],
        compiler_params=_cparams(("parallel", "arbitrary")),
        name="seq_prompt",
    )(u, hp, cw, cb, cg, cbeta, pw, ps)


def _seq_sample_kernel(cext_ref, pext_ref, cw_ref, cb_ref, cg_ref, cbeta_ref, pw_ref, ps_ref,
                       o_ref, *, dec_seq, conv_w, conv_ch, pool_ch, pos0):
    hist = pext_ref.shape[0] - dec_seq
    for t in range(dec_seq):
        acc = None
        for k in range(conv_w):
            term = cext_ref[t + k] * cw_ref[k:k + 1, :]
            acc = term if acc is None else acc + term
        y = _layer_norm(acc + cb_ref[...], cg_ref[...], cbeta_ref[...])
        o_ref[t, :, 0:conv_ch] = (y * _sigmoid(y)).astype(BF16)

        v = pext_ref[hist + t]
        sums = []
        run = None
        i = 0
        for w in POOL_WINDOWS:
            while i < w:
                row = pext_ref[hist + t - i]
                run = row if run is None else run + row
                i += 1
            sums.append(run / float(min(pos0 + t + 1, w)))
        mean = _pool_lane_select(v.shape, sums)
        d = (mean - v).astype(BF16)
        o_ref[t, :, conv_ch:conv_ch + pool_ch] = (_dot(d, pw_ref[...]) * ps_ref[...]).astype(BF16)


def _seq_sample(cext, pext, cw, cb, cg, cbeta, pw, ps, dec_seq, pos0, dm):
    conv_ch, pool_ch = dm['conv_ch'], dm['pool_ch']
    db = cext.shape[1]
    kern = functools.partial(_seq_sample_kernel, dec_seq=dec_seq, conv_w=cw.shape[0],
                             conv_ch=conv_ch, pool_ch=pool_ch, pos0=pos0)
    args = (cext, pext, cw, cb, cg, cbeta, pw, ps)
    return pl.pallas_call(
        kern,
        grid=(1,),
        in_specs=[_const_spec(a.shape) for a in args],
        out_specs=_const_spec((dec_seq, db, conv_ch + pool_ch)),
        out_shape=jax.ShapeDtypeStruct((dec_seq, db, conv_ch + pool_ch), BF16),
        compiler_params=_cparams(("arbitrary",)),
        name="seq_sample",
    )(*args)


def _out_proj_kernel(*refs, heads, att_width, alpha, route):
    if route:
        (olat_ref, ocp_ref, x_ref, wuv_ref, wout_ref, g_ref, b_ref, wrh_ref, wrl_ref, br_ref,
         x1_ref, x1b_ref, route_ref) = refs
    else:
        (olat_ref, ocp_ref, x_ref, wuv_ref, wout_ref, g_ref, b_ref, x1_ref, x1b_ref) = refs
    parts = [_dot(olat_ref[hd], wuv_ref[hd]) for hd in range(heads)]
    o_att = jnp.concatenate(parts, axis=1).astype(BF16)
    mix = _dot(o_att, wout_ref[0:att_width, :]) + _dot(ocp_ref[...], wout_ref[att_width:, :])
    x1 = _layer_norm(alpha * x_ref[...] + mix, g_ref[...], b_ref[...])
    x1_ref[...] = x1
    x1h = x1.astype(BF16)
    x1b_ref[...] = x1h
    if route:
        x1l = (x1 - x1h.astype(F32)).astype(BF16)
        logits = (_dot(x1h, wrh_ref[...]) + _dot(x1l, wrh_ref[...]) + _dot(x1h, wrl_ref[...])
                  + br_ref[...])
        lane = lax.broadcasted_iota(jnp.int32, logits.shape, 1)
        m1 = jnp.max(logits, axis=1, keepdims=True)
        i1 = jnp.min(jnp.where(logits == m1, lane, LANES), axis=1, keepdims=True)
        rest = jnp.where(lane == i1, -jnp.inf, logits)
        m2 = jnp.max(rest, axis=1, keepdims=True)
        i2 = jnp.min(jnp.where(rest == m2, lane, LANES), axis=1, keepdims=True)
        e2 = jnp.exp(m2 - m1)
        g1 = 1.0 / (1.0 + e2)
        g2 = e2 / (1.0 + e2)
        packed = jnp.where(lane == 0, i1.astype(F32),
                           jnp.where(lane == 1, i2.astype(F32),
                                     jnp.where(lane == 2, g1, jnp.where(lane == 3, g2, 0.0))))
        route_ref[...] = packed[:, :route_ref.shape[1]]


def _out_proj(olat, ocp, x, wuv, wout, g, b, router, dm):
    n, d = x.shape
    tm = TOKEN_TILE
    heads, kv_rank = dm['heads'], dm['kv_rank']
    row = lambda i: (i, 0)
    args = [olat, ocp, x, wuv, wout, g, b]
    in_specs = [
        pl.BlockSpec((heads, tm, kv_rank), lambda i: (0, i, 0)),
        pl.BlockSpec((tm, ocp.shape[1]), row),
        pl.BlockSpec((tm, d), row),
        _const_spec(wuv.shape), _const_spec(wout.shape), _const_spec(g.shape), _const_spec(b.shape),
    ]
    out_specs = [pl.BlockSpec((tm, d), row), pl.BlockSpec((tm, d), row)]
    out_shape = [jax.ShapeDtypeStruct((n, d), F32), jax.ShapeDtypeStruct((n, d), BF16)]
    if router is not None:
        wrh, wrl, br = router
        args += [wrh, wrl, br]
        in_specs += [_const_spec(wrh.shape), _const_spec(wrl.shape), _const_spec(br.shape)]
        out_specs.append(pl.BlockSpec((tm, 8), row))
        out_shape.append(jax.ShapeDtypeStruct((n, 8), F32))
    kern = functools.partial(_out_proj_kernel, heads=heads, att_width=heads * dm['v_dim'],
                             alpha=dm['alpha'], route=router is not None)
    return pl.pallas_call(
        kern, grid=(n // tm,), in_specs=in_specs, out_specs=out_specs, out_shape=out_shape,
        compiler_params=_cparams(("parallel",)), name="out_proj",
    )(*args)


def _ff_chunks(d_ff):
    bounds, c0 = [], 0
    while c0 < d_ff:
        c1 = min(c0 + FF_CHUNK, d_ff)
        bounds.append((c0, c1))
        c0 = c1
    return bounds


def _swiglu_tile(xb, wg_ref, wu_ref, wd_ref, d_ff, lead=()):
    y = None
    for c0, c1 in _ff_chunks(d_ff):
        gate = _dot(xb, wg_ref[lead + (slice(None), slice(c0, c1))])
        up = _dot(xb, wu_ref[lead + (slice(None), slice(c0, c1))])
        act = (gate * _sigmoid(gate) * up).astype(BF16)
        part = _dot(act, wd_ref[lead + (slice(c0, c1), slice(None))])
        y = part if y is None else y + part
    return y


def _store_token_rows(y, o_refs, prompt_tiles):
    if len(o_refs) == 1:
        o_refs[0][...] = y
        return

    @pl.when(pl.program_id(0) < prompt_tiles)
    def _():
        o_refs[0][...] = y

    @pl.when(pl.program_id(0) >= prompt_tiles)
    def _():
        o_refs[1][...] = y


def _token_out(n, d, n_prompt, split):
    tm = TOKEN_TILE
    if not split:
        return pl.BlockSpec((tm, d), lambda i, *_: (i, 0)), jax.ShapeDtypeStruct((n, d), F32)
    pt = n_prompt // tm
    specs = [pl.BlockSpec((tm, d), lambda i, *_: (jnp.minimum(i, pt - 1), 0)),
             pl.BlockSpec((tm, d), lambda i, *_: (jnp.maximum(i - pt, 0), 0))]
    shapes = [jax.ShapeDtypeStruct((n_prompt, d), F32), jax.ShapeDtypeStruct((n - n_prompt, d), F32)]
    return specs, shapes


def _ffn_dense_kernel(xb_ref, x_ref, wg_ref, wu_ref, wd_ref, g_ref, b_ref, *o_refs,
                      d_ff, alpha, prompt_tiles):
    y = _swiglu_tile(xb_ref[...], wg_ref, wu_ref, wd_ref, d_ff)
    _store_token_rows(_layer_norm(alpha * x_ref[...] + y, g_ref[...], b_ref[...]), o_refs, prompt_tiles)


def _ffn_dense(x1b, x1, wg, wu, wd, g, b, n_prompt, split, dm):
    n, d = x1.shape
    tm = TOKEN_TILE
    row = lambda i: (i, 0)
    kern = functools.partial(_ffn_dense_kernel, d_ff=wg.shape[1], alpha=dm['alpha'],
                             prompt_tiles=n_prompt // tm)
    out_specs, out_shape = _token_out(n, d, n_prompt, split)
    return pl.pallas_call(
        kern,
        grid=(n // tm,),
        in_specs=[pl.BlockSpec((tm, d), row), pl.BlockSpec((tm, d), row),
                  _resident_spec(wg.shape), _resident_spec(wu.shape), _resident_spec(wd.shape),
                  _const_spec(g.shape), _const_spec(b.shape)],
        out_specs=out_specs,
        out_shape=out_shape,
        compiler_params=_cparams(("arbitrary",)),
        name="ffn_dense",
    )(x1b, x1, wg, wu, wd, g, b)


def _moe_experts_kernel(te_ref, nu_ref, xs_ref, wg_ref, wu_ref, wd_ref, o_ref, *, d_ff):
    del te_ref

    @pl.when(pl.program_id(0) < nu_ref[0])
    def _():
        o_ref[...] = _swiglu_tile(xs_ref[...], wg_ref, wu_ref, wd_ref, d_ff, lead=(0,)).astype(o_ref.dtype)

    @pl.when(pl.program_id(0) >= nu_ref[0])
    def _():
        o_ref[...] = jnp.zeros(o_ref.shape, o_ref.dtype)


def _moe_experts(tile_expert, n_used, xs, wg, wu, wd):
    rows, d = xs.shape
    tm = TOKEN_TILE
    d_ff = wg.shape[2]
    wspec = lambda shape: pl.BlockSpec((1,) + shape[1:], lambda i, te, nu: (te[i], 0, 0),
                                       pipeline_mode=pl.Buffered(1))
    grid_spec = pltpu.PrefetchScalarGridSpec(
        num_scalar_prefetch=2,
        grid=(rows // tm,),
        in_specs=[pl.BlockSpec((tm, d), lambda i, te, nu: (i, 0)),
                  wspec(wg.shape), wspec(wu.shape), wspec(wd.shape)],
        out_specs=pl.BlockSpec((tm, d), lambda i, te, nu: (i, 0)),
    )
    return pl.pallas_call(
        functools.partial(_moe_experts_kernel, d_ff=d_ff),
        grid_spec=grid_spec,
        out_shape=jax.ShapeDtypeStruct((rows, d), BF16),
        compiler_params=_cparams(("arbitrary",)),
        name="moe_experts",
    )(tile_expert, n_used, xs, wg, wu, wd)


def _moe_combine_kernel(x_ref, o1_ref, o2_ref, route_ref, g_ref, b_ref, *o_refs, alpha, prompt_tiles):
    route = route_ref[...]
    y = route[:, 2:3] * o1_ref[...].astype(F32) + route[:, 3:4] * o2_ref[...].astype(F32)
    _store_token_rows(_layer_norm(alpha * x_ref[...] + y, g_ref[...], b_ref[...]), o_refs, prompt_tiles)


def _moe_combine(x1, o1, o2, route, g, b, n_prompt, split, dm):
    n, d = x1.shape
    tm = TOKEN_TILE
    row = lambda i: (i, 0)
    out_specs, out_shape = _token_out(n, d, n_prompt, split)
    return pl.pallas_call(
        functools.partial(_moe_combine_kernel, alpha=dm['alpha'], prompt_tiles=n_prompt // tm),
        grid=(n // tm,),
        in_specs=[pl.BlockSpec((tm, d), row), pl.BlockSpec((tm, d), row), pl.BlockSpec((tm, d), row),
                  pl.BlockSpec((tm, route.shape[1]), row), _const_spec(g.shape), _const_spec(b.shape)],
        out_specs=out_specs,
        out_shape=out_shape,
        compiler_params=_cparams(("arbitrary",)),
        name="moe_combine",
    )(x1, o1, o2, route, g, b)


def _moe_layer(x1, x1b, route, wg, wu, wd, g, b, n_prompt, split, dm):
    n = x1.shape[0]
    n_exp = wg.shape[0]
    tm = TOKEN_TILE
    n_assign = TOP_K * n
    n_tiles = n_assign // tm + n_exp
    e_flat = jnp.concatenate([route[:, 0], route[:, 1]]).astype(jnp.int32)
    onehot = (e_flat[:, None] == jnp.arange(n_exp, dtype=jnp.int32)[None, :]).astype(jnp.int32)
    cnt = jnp.sum(onehot, axis=0)
    rank = jnp.sum((jnp.cumsum(onehot, axis=0) - onehot) * onehot, axis=1)
    padded = ((cnt + tm - 1) // tm) * tm
    pad_end = jnp.cumsum(padded)
    pad_start = pad_end - padded
    cnt_start = jnp.cumsum(cnt) - cnt
    pos = pad_start[e_flat] + rank
    order = jnp.argsort(e_flat, stable=True).astype(jnp.int32)
    slot = jnp.arange(n_tiles * tm, dtype=jnp.int32)
    tile_expert = jnp.minimum(
        jnp.searchsorted(pad_end, jnp.arange(n_tiles, dtype=jnp.int32) * tm, side='right'),
        n_exp - 1).astype(jnp.int32)
    slot_e = jnp.repeat(tile_expert, tm)
    local = slot - pad_start[slot_e]
    src = jnp.where(local < cnt[slot_e], order[jnp.minimum(cnt_start[slot_e] + local, n_assign - 1)], 0)
    tok_sorted = jnp.where(src >= n, src - n, src)
    n_used = (pad_end[-1:] // tm).astype(jnp.int32)

    take = lambda a, idx: a.at[idx].get(mode='promise_in_bounds')
    xs = take(x1b, tok_sorted)
    out_sorted = _moe_experts(tile_expert, n_used, xs, wg, wu, wd)
    return _moe_combine(x1, take(out_sorted, pos[:n]), take(out_sorted, pos[n:]), route, g, b,
                        n_prompt, split, dm)


def _prep_in_weights(w_in, dm):
    q_rank, kv_rank, rope, conv_ch, pool_ch = (dm['q_rank'], dm['kv_rank'], dm['rope'],
                                               dm['conv_ch'], dm['pool_ch'])
    a = q_rank + kv_rank
    w_q, w_kv = w_in[..., :q_rank], w_in[..., q_rank:a]
    w_pe = w_in[..., a:a + rope]
    w_glu = w_in[..., a + rope:a + rope + 2 * conv_ch]
    w_pool = w_in[..., a + rope + 2 * conv_ch:]
    half = rope // 2
    w_sw = jnp.concatenate([-w_pe[..., half:], w_pe[..., :half]], axis=-1)
    zpad = jnp.zeros(w_pe.shape[:-1] + (LANES - rope,), w_in.dtype)
    return jnp.concatenate([w_q, w_kv, w_glu, w_pool, w_pe, zpad, w_sw, zpad], axis=-1).astype(BF16)


def _prep_q_weights(w_q_b, dm):
    depth, q_rank, heads, _ = w_q_b.shape
    nope, rope = dm['nope'], dm['rope']
    half = rope // 2
    w_nope = w_q_b[..., :nope].reshape(depth, q_rank, heads * nope)
    w_pe = w_q_b[..., nope:]
    w_sw = jnp.concatenate([-w_pe[..., half:], w_pe[..., :half]], axis=-1)
    zpad = jnp.zeros(w_pe.shape[:-1] + (LANES - rope,), w_q_b.dtype)
    pad = lambda w: jnp.concatenate([w, zpad], axis=-1).reshape(depth, q_rank, heads * LANES)
    return jnp.concatenate([w_nope, pad(w_pe), pad(w_sw)], axis=-1).astype(BF16)


def _rope_tables(seq, dec_seq, dec_tokens, past, rope):
    half = rope // 2
    inv = ROPE_BASE ** (-jnp.arange(half, dtype=F32) / half)

    def tab(pos):
        ang = pos.astype(F32)[:, None] * inv[None, :]
        z = jnp.zeros((pos.shape[0], LANES - rope), F32)
        c, s = jnp.cos(ang), jnp.sin(ang)
        return jnp.concatenate([c, c, z], axis=1), jnp.concatenate([s, s, z], axis=1)

    cp, sp = tab(jnp.arange(seq, dtype=jnp.int32))
    cs, ss = tab(past + jnp.arange(dec_seq, dtype=jnp.int32))
    reps = dec_tokens // dec_seq
    return (jnp.concatenate([cp, jnp.tile(cs, (reps, 1))], axis=0),
            jnp.concatenate([sp, jnp.tile(ss, (reps, 1))], axis=0))


def kernel(x_prompt, x_sample, cache_ckv, cache_kpe, state_conv, state_pool, page_table, w_in, q_norm_g, kv_norm_g, w_q_b, w_uk, w_uv, conv_w, conv_b, conv_ln_g, conv_ln_b, pool_w, pool_scale, w_out, ln1_g, ln1_b, w_ffn_gate, w_ffn_up, w_ffn_down, w_router, b_router, w_exp_gate, w_exp_up, w_exp_down, ln2_g, ln2_b):
    batch, seq, d_model = x_prompt.shape
    dec_batch, dec_seq, _ = x_sample.shape
    depth = w_in.shape[0]
    q_rank, heads, qk = w_q_b.shape[1:]
    kv_rank, _, nope = w_uk.shape[1:]
    v_dim = w_uv.shape[3]
    rope = qk - nope
    conv_ch = conv_w.shape[2]
    pool_ch = pool_scale.shape[1]
    conv_past = conv_w.shape[1] - 1
    pool_past = state_pool.shape[2]
    n_groups, pool_gc = pool_w.shape[1], pool_w.shape[2]
    n_exp = w_router.shape[2]
    past = page_table.shape[1] * PAGE_SIZE
    n_p, n_s = batch * seq, dec_batch * dec_seq
    tm = TOKEN_TILE
    assert n_p % tm == 0 and n_s % tm == 0 and seq % tm == 0
    assert cache_ckv.shape[2] == PAGE_SIZE and n_groups == len(POOL_WINDOWS)
    assert dec_batch % DECODE_GROUP == 0 and n_p % (DECODE_GROUP * dec_seq) == 0
    assert dec_seq & (dec_seq - 1) == 0 and ATTN_TILE & (ATTN_TILE - 1) == 0
    dm = dict(heads=heads, q_rank=q_rank, kv_rank=kv_rank, nope=nope, rope=rope, v_dim=v_dim,
              conv_ch=conv_ch, pool_ch=pool_ch, q_scale=float(qk) ** -0.5 * LOG2_E,
              alpha=float(2 * depth) ** 0.25)

    win_all = _prep_in_weights(w_in, dm)
    wq_all = _prep_q_weights(w_q_b, dm)
    wuk_all = jnp.transpose(w_uk, (0, 2, 3, 1)).astype(BF16)
    wuv_all = jnp.transpose(w_uv, (0, 2, 1, 3)).astype(BF16)
    eye = jnp.eye(n_groups, dtype=pool_w.dtype)
    pw_all = jnp.einsum('lgcd,gh->lgchd', pool_w, eye).reshape(depth, pool_ch, pool_ch).astype(BF16)
    wout_all = w_out.astype(BF16)
    wfg, wfu, wfd = w_ffn_gate.astype(BF16), w_ffn_up.astype(BF16), w_ffn_down.astype(BF16)
    weg, weu, wed = w_exp_gate.astype(BF16), w_exp_up.astype(BF16), w_exp_down.astype(BF16)
    wr_all = jnp.pad(w_router, ((0, 0), (0, 0), (0, LANES - n_exp)))
    wrh_all = wr_all.astype(BF16)
    wrl_all = (wr_all - wrh_all.astype(F32)).astype(BF16)
    br_all = jnp.pad(b_router, ((0, 0), (0, LANES - n_exp)), constant_values=NEG_BIG)[:, None, :]
    cache_kpe_t = jnp.swapaxes(cache_kpe, 2, 3)
    vec = lambda a, l: a[l][None, :]

    cos_tab, sin_tab = _rope_tables(seq, dec_seq, n_s, past, rope)
    tiles_per_seq, prompt_tiles = seq // tm, n_p // tm
    tab_index = lambda i: jnp.where(i < prompt_tiles, lax.rem(i, tiles_per_seq),
                                    tiles_per_seq + i - prompt_tiles)

    x = jnp.concatenate([x_prompt.reshape(n_p, d_model), x_sample.reshape(n_s, d_model)], axis=0)
    outs = [[] for _ in range(8)]
    for l in range(depth):
        qcat, kcat, ckv_p, kpe_p, ckv_s, kpe_s, u, hp = _in_proj(
            x, win_all[l], wq_all[l], wuk_all[l], vec(q_norm_g, l), vec(kv_norm_g, l),
            cos_tab, sin_tab, tab_index, n_p, dm)

        olat = _prompt_attn(qcat, kcat, batch, seq, dm)
        olat = _decode_attn(page_table, qcat, kcat, cache_ckv, cache_kpe_t, olat, l,
                            n_p, dec_batch, dec_seq, dm)

        seq_w = (conv_w[l], vec(conv_b, l), vec(conv_ln_g, l), vec(conv_ln_b, l),
                 pw_all[l], vec(pool_scale, l))
        ocp = _seq_prompt(u, hp, *seq_w, batch, seq, dm)
        cext = jnp.concatenate([state_conv[l], u[n_p:].reshape(dec_batch, dec_seq, conv_ch)], axis=1)
        pext = jnp.concatenate([state_pool[l], hp[n_p:].reshape(dec_batch, dec_seq, pool_ch)], axis=1)
        ocp_s = _seq_sample(jnp.swapaxes(cext, 0, 1), jnp.swapaxes(pext, 0, 1), *seq_w,
                            dec_seq, past, dm)
        ocp = lax.dynamic_update_slice(
            ocp, jnp.swapaxes(ocp_s, 0, 1).reshape(n_s, conv_ch + pool_ch), (n_p, 0))

        i = l // 2
        router = (wrh_all[i], wrl_all[i], br_all[i]) if l % 2 == 1 else None
        res = _out_proj(olat, ocp, x, wuv_all[l], wout_all[l], vec(ln1_g, l), vec(ln1_b, l),
                        router, dm)
        last = l == depth - 1
        if l % 2 == 0:
            x1, x1b = res
            x = _ffn_dense(x1b, x1, wfg[i], wfu[i], wfd[i], vec(ln2_g, l), vec(ln2_b, l),
                           n_p, last, dm)
        else:
            x1, x1b, route = res
            x = _moe_layer(x1, x1b, route, weg[i], weu[i], wed[i], vec(ln2_g, l), vec(ln2_b, l),
                           n_p, last, dm)

        outs[0].append(ckv_p.reshape(batch, seq, kv_rank))
        outs[1].append(kpe_p.reshape(batch, seq, rope))
        outs[2].append(u[:n_p].reshape(batch, seq, conv_ch)[:, seq - conv_past:])
        outs[3].append(hp[:n_p].reshape(batch, seq, pool_ch)[:, seq - pool_past:])
        outs[4].append(ckv_s.reshape(dec_batch, dec_seq, kv_rank))
        outs[5].append(kpe_s.reshape(dec_batch, dec_seq, rope))
        outs[6].append(cext[:, dec_seq:])
        outs[7].append(pext[:, dec_seq:])

    y_p, y_s = x
    return (y_p.reshape(batch, seq, d_model), y_s.reshape(dec_batch, dec_seq, d_model),
            *[jnp.stack(o) for o in outs])
```

```python
import functools

import jax
import jax.numpy as jnp
from jax import lax
from jax.experimental import pallas as pl
from jax.experimental.pallas import tpu as pltpu

F32 = jnp.float32
BF16 = jnp.bfloat16

LN_EPS = 1e-5
RMS_EPS = 1e-6
ROPE_BASE = 10000.0
PAGE_SIZE = 128
TOP_K = 2
POOL_WINDOWS = (2, 4, 8, 16)

LANES = 128
SUBLANES = 8
V7X_VMEM_LIMIT = 56 * 1024 * 1024
TOKEN_TILE = 512
ATTN_TILE = 512
SEQ_TILE = 256
CONV_CHUNK = 64
DECODE_GROUP = 4
MXU_TILE = 256
FF_CHUNK = 4 * MXU_TILE
NEG_BIG = -1e30
LOG2_E = 1.4426950408889634


def _cparams(sem):
    return pltpu.CompilerParams(dimension_semantics=sem, vmem_limit_bytes=V7X_VMEM_LIMIT)


def _dot(a, b):
    return jnp.dot(a, b, preferred_element_type=F32)


def _dot_nt(a, b):
    return lax.dot_general(a, b, (((1,), (1,)), ((), ())), preferred_element_type=F32)


def _lane_repeat(x, n):
    return x if n == 1 else jnp.concatenate([x] * n, axis=1)


def _sigmoid(x):
    return 1.0 / (1.0 + jnp.exp(-x))


def _layer_norm(y, g, b):
    mu = jnp.mean(y, axis=-1, keepdims=True)
    yc = y - mu
    var = jnp.mean(yc * yc, axis=-1, keepdims=True)
    return yc * lax.rsqrt(var + LN_EPS) * g + b


def _rms_norm(h, g):
    return h * lax.rsqrt(jnp.mean(h * h, axis=-1, keepdims=True) + RMS_EPS) * g


def _const_spec(shape):
    nd = len(shape)
    return pl.BlockSpec(shape, lambda *_: (0,) * nd)


def _resident_spec(shape):
    nd = len(shape)
    return pl.BlockSpec(shape, lambda *_: (0,) * nd, pipeline_mode=pl.Buffered(1))


def _pick_rows(prompt_ref, sample_ref, prompt_tiles):
    return jnp.where(pl.program_id(0) < prompt_tiles, prompt_ref[...], sample_ref[...])


def _in_proj_kernel(xp_ref, xs_ref, win_ref, wq_ref, wuk_ref, wkn_ref, qg_ref, kvg_ref, cos_ref, sin_ref,
                    qh_ref, kh_ref, qcat_ref, kcat_ref, ckvp_ref, kpep_ref, ckvs_ref, kpes_ref,
                    u_ref, hp_ref,
                    *, heads, q_rank, kv_rank, conv_ch, pool_ch, nope, rope, scale, prompt_tiles):
    x = _pick_rows(xp_ref, xs_ref, prompt_tiles).astype(BF16)
    h = _dot(x, win_ref[...])
    o_kv = q_rank
    o_glu = o_kv + kv_rank
    o_pool = o_glu + 2 * conv_ch
    o_pe = o_pool + pool_ch
    o_sw = o_pe + LANES
    cos = cos_ref[...]
    sin = sin_ref[...]

    qn = _rms_norm(h[:, :q_rank], qg_ref[...]).astype(BF16)
    q = _dot(qn, wq_ref[...])
    o_qpe = heads * nope
    o_qsw = o_qpe + heads * LANES
    ckv = _rms_norm(h[:, o_kv:o_kv + kv_rank], kvg_ref[...])
    kpe = h[:, o_pe:o_pe + LANES] * cos + h[:, o_sw:o_sw + LANES] * sin
    ckv_b = ckv.astype(BF16)
    kpe_b = kpe.astype(BF16)
    kcat_ref[:, 0:kv_rank] = ckv_b
    kcat_ref[:, kv_rank:] = kpe_b
    k_nope = _dot(ckv_b, wkn_ref[...])

    q_pe = []
    for hd in range(heads):
        pe = q[:, o_qpe + hd * LANES:o_qpe + (hd + 1) * LANES]
        sw = q[:, o_qsw + hd * LANES:o_qsw + (hd + 1) * LANES]
        q_pe.append(((pe * cos + sw * sin) * scale).astype(BF16))
        qh_ref[hd, :, 0:nope] = (q[:, hd * nope:(hd + 1) * nope] * scale).astype(BF16)
        qh_ref[hd, :, nope:] = q_pe[hd]
        kh_ref[hd, :, 0:nope] = k_nope[:, hd * nope:(hd + 1) * nope].astype(BF16)
        kh_ref[hd, :, nope:] = kpe_b

    for hd in range(heads):
        q_nope = q[:, hd * nope:(hd + 1) * nope].astype(BF16)
        qcat_ref[hd, :, 0:kv_rank] = (_dot(q_nope, wuk_ref[hd]) * scale).astype(BF16)
        qcat_ref[hd, :, kv_rank:] = q_pe[hd]

    @pl.when(pl.program_id(0) < prompt_tiles)
    def _():
        ckvp_ref[...] = ckv
        kpep_ref[...] = kpe[:, :rope]

    @pl.when(pl.program_id(0) >= prompt_tiles)
    def _():
        ckvs_ref[...] = ckv
        kpes_ref[...] = kpe[:, :rope]

    u_ref[...] = h[:, o_glu:o_glu + conv_ch] * _sigmoid(h[:, o_glu + conv_ch:o_glu + 2 * conv_ch])
    hp_ref[...] = h[:, o_pool:o_pool + pool_ch]


def _in_proj(xp, xs, win, wq, wuk, wkn, qg, kvg, cos_tab, sin_tab, tab_index, n_prompt, dm):
    d = xp.shape[1]
    n = n_prompt + xs.shape[0]
    tm = TOKEN_TILE
    heads, q_rank, kv_rank = dm['heads'], dm['q_rank'], dm['kv_rank']
    conv_ch, pool_ch, rope, nope = dm['conv_ch'], dm['pool_ch'], dm['rope'], dm['nope']
    prompt_tiles = n_prompt // tm
    row = lambda i: (i, 0)
    hrow = lambda i: (0, i, 0)
    prow = lambda i: (jnp.minimum(i, prompt_tiles - 1), 0)
    srow = lambda i: (jnp.maximum(i - prompt_tiles, 0), 0)
    hsrow = lambda i: (0, jnp.maximum(i - prompt_tiles, 0), 0)
    kern = functools.partial(_in_proj_kernel, heads=heads, q_rank=q_rank, kv_rank=kv_rank,
                             conv_ch=conv_ch, pool_ch=pool_ch, nope=dm['nope'], rope=rope,
                             scale=dm['q_scale'], prompt_tiles=prompt_tiles)
    return pl.pallas_call(
        kern,
        grid=(n // tm,),
        in_specs=[
            pl.BlockSpec((tm, d), prow), pl.BlockSpec((tm, d), srow),
            _const_spec(win.shape), _const_spec(wq.shape), _const_spec(wuk.shape),
            _const_spec(wkn.shape), _const_spec(qg.shape), _const_spec(kvg.shape),
            pl.BlockSpec((tm, LANES), lambda i: (tab_index(i), 0)),
            pl.BlockSpec((tm, LANES), lambda i: (tab_index(i), 0)),
        ],
        out_specs=[
            pl.BlockSpec((heads, tm, nope + LANES), hrow),
            pl.BlockSpec((heads, tm, nope + LANES), hrow),
            pl.BlockSpec((heads, tm, kv_rank + LANES), hsrow),
            pl.BlockSpec((tm, kv_rank + LANES), row),
            pl.BlockSpec((tm, kv_rank), prow),
            pl.BlockSpec((tm, rope), prow),
            pl.BlockSpec((tm, kv_rank), srow),
            pl.BlockSpec((tm, rope), srow),
            pl.BlockSpec((tm, conv_ch), row),
            pl.BlockSpec((tm, pool_ch), row),
        ],
        out_shape=[
            jax.ShapeDtypeStruct((heads, n, nope + LANES), BF16),
            jax.ShapeDtypeStruct((heads, n, nope + LANES), BF16),
            jax.ShapeDtypeStruct((heads, n - n_prompt, kv_rank + LANES), BF16),
            jax.ShapeDtypeStruct((n, kv_rank + LANES), BF16),
            jax.ShapeDtypeStruct((n_prompt, kv_rank), F32),
            jax.ShapeDtypeStruct((n_prompt, rope), F32),
            jax.ShapeDtypeStruct((n - n_prompt, kv_rank), F32),
            jax.ShapeDtypeStruct((n - n_prompt, rope), F32),
            jax.ShapeDtypeStruct((n, conv_ch), F32),
            jax.ShapeDtypeStruct((n, pool_ch), F32),
        ],
        compiler_params=_cparams(("arbitrary",)),
        name="in_proj",
    )(xp, xs, win, wq, wuk, wkn, qg, kvg, cos_tab, sin_tab)


def _prompt_attn_kernel(q_ref, kh_ref, kc_ref, o_ref, m_sc, l_sc, acc_sc, *, heads, tile, kv_rank):
    qi = pl.program_id(1)
    m_sc[...] = jnp.full(m_sc.shape, -jnp.inf, F32)
    l_sc[...] = jnp.zeros(l_sc.shape, F32)
    acc_sc[...] = jnp.zeros(acc_sc.shape, F32)

    def step(ki, masked):
        start = pl.multiple_of(ki * tile, tile)
        kc = kc_ref[pl.ds(start, tile), 0:kv_rank]
        for hd in range(heads):
            rows = pl.ds(hd * tile, tile)
            s = _dot_nt(q_ref[hd], kh_ref[hd, pl.ds(start, tile), :])
            if masked:
                r = lax.broadcasted_iota(jnp.int32, s.shape, 0)
                c = lax.broadcasted_iota(jnp.int32, s.shape, 1)
                s = jnp.where(c <= r, s, -jnp.inf)
            m_prev = m_sc[rows, :]
            m_new = jnp.maximum(m_prev, jnp.max(s, axis=1, keepdims=True))
            p = jnp.exp2(s - _lane_repeat(m_new, tile // LANES))
            alpha = jnp.exp2(m_prev - m_new)
            l_sc[rows, :] = alpha * l_sc[rows, :] + jnp.sum(p, axis=1, keepdims=True)
            acc_sc[rows, :] = (acc_sc[rows, :] * _lane_repeat(alpha, kv_rank // LANES)
                               + _dot(p.astype(BF16), kc))
            m_sc[rows, :] = m_new

    def body(ki, carry):
        step(ki, False)
        return carry

    lax.fori_loop(0, qi, body, 0)
    step(qi, True)
    inv = 1.0 / l_sc[...]
    o = acc_sc[...] * _lane_repeat(inv, kv_rank // LANES)
    o_ref[...] = o.reshape(heads, tile, kv_rank).astype(BF16)


def _prompt_attn(qh, kh, kcat, batch, seq, dm):
    heads, kv_rank = dm['heads'], dm['kv_rank']
    width = qh.shape[2]
    tile = min(ATTN_TILE, seq)
    nq = seq // tile
    rows = heads * tile
    kern = functools.partial(_prompt_attn_kernel, heads=heads, tile=tile, kv_rank=kv_rank)
    qblk = lambda b, i: (0, b * nq + i, 0)
    return pl.pallas_call(
        kern,
        grid=(batch, nq),
        in_specs=[
            pl.BlockSpec((heads, tile, width), qblk),
            pl.BlockSpec((heads, seq, width), lambda b, i: (0, b, 0)),
            pl.BlockSpec((seq, kcat.shape[1]), lambda b, i: (b, 0)),
        ],
        out_specs=pl.BlockSpec((heads, tile, kv_rank), qblk),
        out_shape=jax.ShapeDtypeStruct((heads, batch * seq, kv_rank), BF16),
        scratch_shapes=[
            pltpu.VMEM((rows, LANES), F32),
            pltpu.VMEM((rows, LANES), F32),
            pltpu.VMEM((rows, kv_rank), F32),
        ],
        compiler_params=_cparams(("parallel", "arbitrary")),
        name="prompt_attn",
    )(qh, kh, kcat)


def _decode_attn_kernel(pt_ref, q_ref, kn_ref, cache_c, cache_p,
                        o_ref, kbuf, pbuf, sem,
                        *, layer, heads, group, dec_seq, n_pages, kv_rank, rope):
    g = pl.program_id(0)
    n_seq = pl.num_programs(0)
    slot = lax.rem(g, 2)
    j = lax.rem(g, group)
    rows_h = group * dec_seq
    rows = heads * rows_h

    def start_seq(seq, sl):
        def body(p, c):
            pg = pt_ref[seq * n_pages + p]
            dst = pl.ds(pl.multiple_of(p * PAGE_SIZE, PAGE_SIZE), PAGE_SIZE)
            pltpu.make_async_copy(cache_c.at[layer, pg], kbuf.at[sl, dst], sem.at[0, sl]).start()
            pltpu.make_async_copy(cache_p.at[layer, pg], pbuf.at[sl, :, dst], sem.at[1, sl]).start()
            return c
        lax.fori_loop(0, n_pages, body, 0, unroll=min(8, n_pages))

    def wait_seq(sl):
        pltpu.make_async_copy(kbuf.at[sl], kbuf.at[sl], sem.at[0, sl]).wait()
        pltpu.make_async_copy(pbuf.at[sl], pbuf.at[sl], sem.at[1, sl]).wait()

    @pl.when(g == 0)
    def _():
        start_seq(g, slot)

    @pl.when(g + 1 < n_seq)
    def _():
        start_seq(g + 1, 1 - slot)

    wait_seq(slot)

    q = q_ref[...].reshape(rows, q_ref.shape[2])
    kb = kbuf[slot].astype(BF16)
    pb = pbuf[slot].astype(BF16)
    s_past = _dot_nt(q[:, :kv_rank], kb) + _dot(q[:, kv_rank:kv_rank + rope], pb)

    kn = kn_ref[...]
    s_new = _dot_nt(q, kn)
    r = lax.broadcasted_iota(jnp.int32, s_new.shape, 0)
    c = lax.broadcasted_iota(jnp.int32, s_new.shape, 1)
    shift = dec_seq.bit_length() - 1
    visible = ((c >> shift) == j) & ((c & (dec_seq - 1)) <= (r & (dec_seq - 1)))
    s_new = jnp.where(visible, s_new, -jnp.inf)

    m = jnp.maximum(jnp.max(s_past, axis=1, keepdims=True), jnp.max(s_new, axis=1, keepdims=True))
    p_past = jnp.exp2(s_past - m)
    p_new = jnp.exp2(s_new - m)
    denom = jnp.sum(p_past, axis=1, keepdims=True) + jnp.sum(p_new, axis=1, keepdims=True)
    o = (_dot(p_past.astype(BF16), kb) + _dot(p_new.astype(BF16), kn[:, :kv_rank])) / denom
    o = o.reshape(heads, rows_h, kv_rank).astype(BF16)

    @pl.when(j == 0)
    def _():
        o_ref[...] = o

    @pl.when(j != 0)
    def _():
        rr = lax.broadcasted_iota(jnp.int32, o.shape, 1)
        o_ref[...] = jnp.where((rr >> shift) == j, o, o_ref[...])


def _decode_attn(page_table, qcat, kcat, cache_ckv, cache_kpe_t, layer,
                 n_prompt, dec_batch, dec_seq, dm):
    heads, kv_rank, rope = dm['heads'], dm['kv_rank'], dm['rope']
    width = qcat.shape[2]
    group = DECODE_GROUP
    rows_h = group * dec_seq
    n_pages = page_table.shape[1]
    past = n_pages * PAGE_SIZE
    base = n_prompt // rows_h
    hrow = lambda g, pt: (0, g // group, 0)
    row = lambda g, pt: (base + g // group, 0)
    kern = functools.partial(_decode_attn_kernel, layer=layer, heads=heads, group=group,
                             dec_seq=dec_seq, n_pages=n_pages, kv_rank=kv_rank, rope=rope)
    grid_spec = pltpu.PrefetchScalarGridSpec(
        num_scalar_prefetch=1,
        grid=(dec_batch,),
        in_specs=[
            pl.BlockSpec((heads, rows_h, width), hrow),
            pl.BlockSpec((rows_h, width), row),
            pl.BlockSpec(memory_space=pl.ANY),
            pl.BlockSpec(memory_space=pl.ANY),
        ],
        out_specs=pl.BlockSpec((heads, rows_h, kv_rank), hrow),
        scratch_shapes=[
            pltpu.VMEM((2, past, kv_rank), F32),
            pltpu.VMEM((2, rope, past), F32),
            pltpu.SemaphoreType.DMA((2, 2)),
        ],
    )
    return pl.pallas_call(
        kern,
        grid_spec=grid_spec,
        out_shape=jax.ShapeDtypeStruct((heads, dec_batch * dec_seq, kv_rank), BF16),
        compiler_params=_cparams(("arbitrary",)),
        name="decode_attn",
    )(page_table.reshape(-1), qcat, kcat, cache_ckv, cache_kpe_t)


def _pool_lane_select(shape, vals):
    lane_group = lax.broadcasted_iota(jnp.int32, shape, len(shape) - 1) // (shape[-1] // len(vals))
    out = vals[-1]
    for gi in range(len(vals) - 2, -1, -1):
        out = jnp.where(lane_group == gi, vals[gi], out)
    return out


def _seq_prompt_kernel(u_ref, hp_ref, cw8_ref, cb_ref, cg_ref, cbeta_ref, pw_ref, ps_ref,
                       o_ref, ctail_ref, ptail_ref, cext, pext, shifted,
                       *, tile, conv_w, conv_ch, pool_ch):
    j = pl.program_id(1)
    chalo = cext.shape[0] - tile
    phalo = pext.shape[0] - tile

    @pl.when(j == 0)
    def _():
        cext[0:chalo, :] = jnp.zeros((chalo, conv_ch), F32)
        pext[0:phalo, :] = jnp.zeros((phalo, pool_ch), F32)

    cext[chalo:, :] = u_ref[...]
    pext[phalo:, :] = hp_ref[...]

    span = cext.shape[0] - SUBLANES
    for b in range(1, SUBLANES):
        shifted[b, 0:span, :] = cext[pl.ds(b, span), :]
    lead = chalo - (conv_w - 1)
    for c0 in range(0, tile, CONV_CHUNK):
        acc = None
        for k in range(conv_w):
            a, b = divmod(lead + k, SUBLANES)
            win = pl.ds(c0 + a * SUBLANES, CONV_CHUNK)
            rows = cext[win, :] if b == 0 else shifted[b, win, :]
            term = rows.reshape(CONV_CHUNK // SUBLANES, SUBLANES, conv_ch) * cw8_ref[k][None]
            acc = term if acc is None else acc + term
        acc = acc.reshape(CONV_CHUNK, conv_ch)
        y = _layer_norm(acc + cb_ref[...], cg_ref[...], cbeta_ref[...])
        o_ref[c0:c0 + CONV_CHUNK, 0:conv_ch] = (y * _sigmoid(y)).astype(BF16)

    xe = pext[...]
    s2 = xe + pltpu.roll(xe, 1, 0)
    s4 = s2 + pltpu.roll(s2, 2, 0)
    s8 = s4 + pltpu.roll(s4, 4, 0)
    s16 = s8 + pltpu.roll(s8, 8, 0)
    sums = [s[phalo:, :] for s in (s2, s4, s8, s16)]
    v = xe[phalo:, :]
    shape = v.shape
    pos = j * tile + lax.broadcasted_iota(jnp.int32, shape, 0)
    win = _pool_lane_select(shape, [jnp.full(shape, w, jnp.int32) for w in POOL_WINDOWS])
    cnt = jnp.minimum(pos + 1, win).astype(F32)
    mean = _pool_lane_select(shape, sums) / cnt
    d = (mean - v).astype(BF16)
    o_ref[:, conv_ch:conv_ch + pool_ch] = (_dot(d, pw_ref[...]) * ps_ref[...]).astype(BF16)

    cext[0:chalo, :] = cext[tile:tile + chalo, :]
    pext[0:phalo, :] = pext[tile:tile + phalo, :]

    @pl.when(j == pl.num_programs(1) - 1)
    def _():
        ctail_ref[0] = cext[tile:tile + chalo, :]
        ptail_ref[0] = pext[tile:tile + phalo, :]


def _seq_prompt(u, hp, cw, cb, cg, cbeta, pw, ps, batch, seq, dm):
    conv_ch, pool_ch = dm['conv_ch'], dm['pool_ch']
    conv_w = cw.shape[0]
    cw = jnp.broadcast_to(cw[:, None, :], (conv_w, SUBLANES, conv_ch))
    tile = min(SEQ_TILE, seq)
    nt = seq // tile
    chalo = -(-(conv_w - 1) // 8) * 8
    phalo = -(-max(POOL_WINDOWS) // 8) * 8
    kern = functools.partial(_seq_prompt_kernel, tile=tile, conv_w=conv_w,
                             conv_ch=conv_ch, pool_ch=pool_ch)
    row = lambda b, j: (b * nt + j, 0)
    tail = lambda b, j: (b, 0, 0)
    return pl.pallas_call(
        kern,
        grid=(batch, nt),
        in_specs=[
            pl.BlockSpec((tile, conv_ch), row),
            pl.BlockSpec((tile, pool_ch), row),
            _const_spec(cw.shape), _const_spec(cb.shape), _const_spec(cg.shape),
            _const_spec(cbeta.shape), _const_spec(pw.shape), _const_spec(ps.shape),
        ],
        out_specs=[pl.BlockSpec((tile, conv_ch + pool_ch), row),
                   pl.BlockSpec((1, chalo, conv_ch), tail),
                   pl.BlockSpec((1, phalo, pool_ch), tail)],
        out_shape=[jax.ShapeDtypeStruct((batch * seq, conv_ch + pool_ch), BF16),
                   jax.ShapeDtypeStruct((batch, chalo, conv_ch), F32),
                   jax.ShapeDtypeStruct((batch, phalo, pool_ch), F32)],
        scratch_shapes=[
            pltpu.VMEM((chalo + tile, conv_ch), F32),
            pltpu.VMEM((phalo + tile, pool_ch), F32),
            pltpu.VMEM((SUBLANES, chalo + tile - SUBLANES, conv_ch), F32),
        ],
        compiler_params=_cparams(("parallel", "arbitrary")),
        name="seq_prompt",
    )(u, hp, cw, cb, cg, cbeta, pw, ps)


def _seq_sample_kernel(cext_ref, pext_ref, cw_ref, cb_ref, cg_ref, cbeta_ref, pw_ref, ps_ref,
                       o_ref, *, dec_seq, conv_w, conv_ch, pool_ch, pos0):
    hist = pext_ref.shape[0] - dec_seq
    for t in range(dec_seq):
        acc = None
        for k in range(conv_w):
            term = cext_ref[t + k] * cw_ref[k:k + 1, :]
            acc = term if acc is None else acc + term
        y = _layer_norm(acc + cb_ref[...], cg_ref[...], cbeta_ref[...])
        o_ref[t, :, 0:conv_ch] = (y * _sigmoid(y)).astype(BF16)

        v = pext_ref[hist + t]
        sums = []
        run = None
        i = 0
        for w in POOL_WINDOWS:
            while i < w:
                row = pext_ref[hist + t - i]
                run = row if run is None else run + row
                i += 1
            sums.append(run / float(min(pos0 + t + 1, w)))
        mean = _pool_lane_select(v.shape, sums)
        d = (mean - v).astype(BF16)
        o_ref[t, :, conv_ch:conv_ch + pool_ch] = (_dot(d, pw_ref[...]) * ps_ref[...]).astype(BF16)


def _seq_sample(cext, pext, cw, cb, cg, cbeta, pw, ps, dec_seq, pos0, dm):
    conv_ch, pool_ch = dm['conv_ch'], dm['pool_ch']
    db = cext.shape[1]
    kern = functools.partial(_seq_sample_kernel, dec_seq=dec_seq, conv_w=cw.shape[0],
                             conv_ch=conv_ch, pool_ch=pool_ch, pos0=pos0)
    args = (cext, pext, cw, cb, cg, cbeta, pw, ps)
    return pl.pallas_call(
        kern,
        grid=(1,),
        in_specs=[_const_spec(a.shape) for a in args],
        out_specs=_const_spec((dec_seq, db, conv_ch + pool_ch)),
        out_shape=jax.ShapeDtypeStruct((dec_seq, db, conv_ch + pool_ch), BF16),
        compiler_params=_cparams(("arbitrary",)),
        name="seq_sample",
    )(*args)


def _out_proj_kernel(*refs, heads, att_width, alpha, route, prompt_tiles):
    if route:
        (olp_ref, ols_ref, ocpp_ref, ocps_ref, xp_ref, xs_ref, wuv_ref, wout_ref, g_ref, b_ref,
         wrh_ref, wrl_ref, br_ref, x1_ref, x1b_ref, route_ref) = refs
    else:
        (olp_ref, ols_ref, ocpp_ref, ocps_ref, xp_ref, xs_ref, wuv_ref, wout_ref, g_ref, b_ref,
         x1_ref, x1b_ref) = refs
    olat = _pick_rows(olp_ref, ols_ref, prompt_tiles)
    parts = [_dot(olat[hd], wuv_ref[hd]) for hd in range(heads)]
    o_att = jnp.concatenate(parts, axis=1).astype(BF16)
    mix = (_dot(o_att, wout_ref[0:att_width, :])
           + _dot(_pick_rows(ocpp_ref, ocps_ref, prompt_tiles), wout_ref[att_width:, :]))
    x1 = _layer_norm(alpha * _pick_rows(xp_ref, xs_ref, prompt_tiles) + mix, g_ref[...], b_ref[...])
    x1_ref[...] = x1
    x1h = x1.astype(BF16)
    x1b_ref[...] = x1h
    if route:
        x1l = (x1 - x1h.astype(F32)).astype(BF16)
        logits = (_dot(x1h, wrh_ref[...]) + _dot(x1l, wrh_ref[...]) + _dot(x1h, wrl_ref[...])
                  + br_ref[...])
        lane = lax.broadcasted_iota(jnp.int32, logits.shape, 1)
        m1 = jnp.max(logits, axis=1, keepdims=True)
        i1 = jnp.min(jnp.where(logits == m1, lane, LANES), axis=1, keepdims=True)
        rest = jnp.where(lane == i1, -jnp.inf, logits)
        m2 = jnp.max(rest, axis=1, keepdims=True)
        i2 = jnp.min(jnp.where(rest == m2, lane, LANES), axis=1, keepdims=True)
        e2 = jnp.exp(m2 - m1)
        g1 = 1.0 / (1.0 + e2)
        g2 = e2 / (1.0 + e2)
        packed = jnp.where(lane == 0, i1.astype(F32),
                           jnp.where(lane == 1, i2.astype(F32),
                                     jnp.where(lane == 2, g1, jnp.where(lane == 3, g2, 0.0))))
        route_ref[...] = packed[:, :route_ref.shape[1]]


def _out_proj(olat_p, olat_s, ocp_p, ocp_s, xp, xs, wuv, wout, g, b, router, n_prompt, dm):
    d = xp.shape[1]
    n = n_prompt + xs.shape[0]
    tm = TOKEN_TILE
    pt = n_prompt // tm
    heads, kv_rank = dm['heads'], dm['kv_rank']
    row = lambda i: (i, 0)
    prow = lambda i: (jnp.minimum(i, pt - 1), 0)
    srow = lambda i: (jnp.maximum(i - pt, 0), 0)
    width = ocp_p.shape[1]
    args = [olat_p, olat_s, ocp_p, ocp_s, xp, xs, wuv, wout, g, b]
    in_specs = [
        pl.BlockSpec((heads, tm, kv_rank), lambda i: (0, jnp.minimum(i, pt - 1), 0)),
        pl.BlockSpec((heads, tm, kv_rank), lambda i: (0, jnp.maximum(i - pt, 0), 0)),
        pl.BlockSpec((tm, width), prow), pl.BlockSpec((tm, width), srow),
        pl.BlockSpec((tm, d), prow), pl.BlockSpec((tm, d), srow),
        _const_spec(wuv.shape), _const_spec(wout.shape), _const_spec(g.shape), _const_spec(b.shape),
    ]
    out_specs = [pl.BlockSpec((tm, d), row), pl.BlockSpec((tm, d), row)]
    out_shape = [jax.ShapeDtypeStruct((n, d), F32), jax.ShapeDtypeStruct((n, d), BF16)]
    if router is not None:
        wrh, wrl, br = router
        args += [wrh, wrl, br]
        in_specs += [_const_spec(wrh.shape), _const_spec(wrl.shape), _const_spec(br.shape)]
        out_specs.append(pl.BlockSpec((tm, 8), row))
        out_shape.append(jax.ShapeDtypeStruct((n, 8), F32))
    kern = functools.partial(_out_proj_kernel, heads=heads, att_width=heads * dm['v_dim'],
                             alpha=dm['alpha'], route=router is not None, prompt_tiles=pt)
    return pl.pallas_call(
        kern, grid=(n // tm,), in_specs=in_specs, out_specs=out_specs, out_shape=out_shape,
        compiler_params=_cparams(("parallel",)), name="out_proj",
    )(*args)


def _ff_chunks(d_ff):
    bounds, c0 = [], 0
    while c0 < d_ff:
        c1 = min(c0 + FF_CHUNK, d_ff)
        bounds.append((c0, c1))
        c0 = c1
    return bounds


def _swiglu_tile(xb, wg_ref, wu_ref, wd_ref, d_ff, lead=()):
    y = None
    for c0, c1 in _ff_chunks(d_ff):
        gate = _dot(xb, wg_ref[lead + (slice(None), slice(c0, c1))])
        up = _dot(xb, wu_ref[lead + (slice(None), slice(c0, c1))])
        act = (gate * _sigmoid(gate) * up).astype(BF16)
        part = _dot(act, wd_ref[lead + (slice(c0, c1), slice(None))])
        y = part if y is None else y + part
    return y


def _store_token_rows(y, o_refs, prompt_tiles):
    if len(o_refs) == 1:
        o_refs[0][...] = y
        return

    @pl.when(pl.program_id(0) < prompt_tiles)
    def _():
        o_refs[0][...] = y

    @pl.when(pl.program_id(0) >= prompt_tiles)
    def _():
        o_refs[1][...] = y


def _token_out(n, d, n_prompt, split):
    tm = TOKEN_TILE
    if not split:
        return pl.BlockSpec((tm, d), lambda i, *_: (i, 0)), jax.ShapeDtypeStruct((n, d), F32)
    pt = n_prompt // tm
    specs = [pl.BlockSpec((tm, d), lambda i, *_: (jnp.minimum(i, pt - 1), 0)),
             pl.BlockSpec((tm, d), lambda i, *_: (jnp.maximum(i - pt, 0), 0))]
    shapes = [jax.ShapeDtypeStruct((n_prompt, d), F32), jax.ShapeDtypeStruct((n - n_prompt, d), F32)]
    return specs, shapes


def _ffn_dense_kernel(xb_ref, x_ref, wg_ref, wu_ref, wd_ref, g_ref, b_ref, *o_refs,
                      d_ff, alpha, prompt_tiles):
    y = _swiglu_tile(xb_ref[...], wg_ref, wu_ref, wd_ref, d_ff)
    _store_token_rows(_layer_norm(alpha * x_ref[...] + y, g_ref[...], b_ref[...]), o_refs, prompt_tiles)


def _ffn_dense(x1b, x1, wg, wu, wd, g, b, n_prompt, split, dm):
    n, d = x1.shape
    tm = TOKEN_TILE
    row = lambda i: (i, 0)
    kern = functools.partial(_ffn_dense_kernel, d_ff=wg.shape[1], alpha=dm['alpha'],
                             prompt_tiles=n_prompt // tm)
    out_specs, out_shape = _token_out(n, d, n_prompt, split)
    return pl.pallas_call(
        kern,
        grid=(n // tm,),
        in_specs=[pl.BlockSpec((tm, d), row), pl.BlockSpec((tm, d), row),
                  _resident_spec(wg.shape), _resident_spec(wu.shape), _resident_spec(wd.shape),
                  _const_spec(g.shape), _const_spec(b.shape)],
        out_specs=out_specs,
        out_shape=out_shape,
        compiler_params=_cparams(("arbitrary",)),
        name="ffn_dense",
    )(x1b, x1, wg, wu, wd, g, b)


def _moe_experts_kernel(te_ref, nu_ref, xs_ref, wg_ref, wu_ref, wd_ref, o_ref, *, d_ff):
    del te_ref

    @pl.when(pl.program_id(0) < nu_ref[0])
    def _():
        o_ref[...] = _swiglu_tile(xs_ref[...], wg_ref, wu_ref, wd_ref, d_ff, lead=(0,)).astype(o_ref.dtype)

    @pl.when(pl.program_id(0) >= nu_ref[0])
    def _():
        o_ref[...] = jnp.zeros(o_ref.shape, o_ref.dtype)


def _moe_experts(tile_expert, n_used, xs, wg, wu, wd, layer):
    rows, d = xs.shape
    tm = TOKEN_TILE
    d_ff = wg.shape[3]
    wspec = lambda shape: pl.BlockSpec((None, 1) + shape[2:], lambda i, te, nu: (layer, te[i], 0, 0),
                                       pipeline_mode=pl.Buffered(1))
    grid_spec = pltpu.PrefetchScalarGridSpec(
        num_scalar_prefetch=2,
        grid=(rows // tm,),
        in_specs=[pl.BlockSpec((tm, d), lambda i, te, nu: (i, 0)),
                  wspec(wg.shape), wspec(wu.shape), wspec(wd.shape)],
        out_specs=pl.BlockSpec((tm, d), lambda i, te, nu: (i, 0)),
    )
    return pl.pallas_call(
        functools.partial(_moe_experts_kernel, d_ff=d_ff),
        grid_spec=grid_spec,
        out_shape=jax.ShapeDtypeStruct((rows, d), BF16),
        compiler_params=_cparams(("arbitrary",)),
        name="moe_experts",
    )(tile_expert, n_used, xs, wg, wu, wd)


def _moe_combine_kernel(x_ref, o1_ref, o2_ref, route_ref, g_ref, b_ref, *o_refs, alpha, prompt_tiles):
    route = route_ref[...]
    y = route[:, 2:3] * o1_ref[...].astype(F32) + route[:, 3:4] * o2_ref[...].astype(F32)
    _store_token_rows(_layer_norm(alpha * x_ref[...] + y, g_ref[...], b_ref[...]), o_refs, prompt_tiles)


def _moe_combine(x1, o1, o2, route, g, b, n_prompt, split, dm):
    n, d = x1.shape
    tm = TOKEN_TILE
    row = lambda i: (i, 0)
    out_specs, out_shape = _token_out(n, d, n_prompt, split)
    return pl.pallas_call(
        functools.partial(_moe_combine_kernel, alpha=dm['alpha'], prompt_tiles=n_prompt // tm),
        grid=(n // tm,),
        in_specs=[pl.BlockSpec((tm, d), row), pl.BlockSpec((tm, d), row), pl.BlockSpec((tm, d), row),
                  pl.BlockSpec((tm, route.shape[1]), row), _const_spec(g.shape), _const_spec(b.shape)],
        out_specs=out_specs,
        out_shape=out_shape,
        compiler_params=_cparams(("arbitrary",)),
        name="moe_combine",
    )(x1, o1, o2, route, g, b)


def _moe_layer(x1, x1b, route, wg, wu, wd, layer, g, b, n_prompt, split, dm):
    n = x1.shape[0]
    n_exp = wg.shape[1]
    tm = TOKEN_TILE
    n_assign = TOP_K * n
    n_tiles = n_assign // tm + n_exp
    e_flat = jnp.concatenate([route[:, 0], route[:, 1]]).astype(jnp.int32)
    onehot = (e_flat[:, None] == jnp.arange(n_exp, dtype=jnp.int32)[None, :]).astype(jnp.int32)
    cnt = jnp.sum(onehot, axis=0)
    rank = jnp.sum((jnp.cumsum(onehot, axis=0) - onehot) * onehot, axis=1)
    padded = ((cnt + tm - 1) // tm) * tm
    pad_end = jnp.cumsum(padded)
    pad_start = pad_end - padded
    cnt_start = jnp.cumsum(cnt) - cnt
    pos = pad_start[e_flat] + rank
    order = jnp.argsort(e_flat, stable=True).astype(jnp.int32)
    slot = jnp.arange(n_tiles * tm, dtype=jnp.int32)
    tile_expert = jnp.minimum(
        jnp.searchsorted(pad_end, jnp.arange(n_tiles, dtype=jnp.int32) * tm, side='right'),
        n_exp - 1).astype(jnp.int32)
    slot_e = jnp.repeat(tile_expert, tm)
    local = slot - pad_start[slot_e]
    src = jnp.where(local < cnt[slot_e], order[jnp.minimum(cnt_start[slot_e] + local, n_assign - 1)], 0)
    tok_sorted = jnp.where(src >= n, src - n, src)
    n_used = (pad_end[-1:] // tm).astype(jnp.int32)

    take = lambda a, idx: a.at[idx].get(mode='promise_in_bounds')
    xs = take(x1b, tok_sorted)
    out_sorted = _moe_experts(tile_expert, n_used, xs, wg, wu, wd, layer)
    return _moe_combine(x1, take(out_sorted, pos[:n]), take(out_sorted, pos[n:]), route, g, b,
                        n_prompt, split, dm)


def _prep_in_weights(w_in, dm):
    q_rank, kv_rank, rope, conv_ch, pool_ch = (dm['q_rank'], dm['kv_rank'], dm['rope'],
                                               dm['conv_ch'], dm['pool_ch'])
    a = q_rank + kv_rank
    w_q, w_kv = w_in[..., :q_rank], w_in[..., q_rank:a]
    w_pe = w_in[..., a:a + rope]
    w_glu = w_in[..., a + rope:a + rope + 2 * conv_ch]
    w_pool = w_in[..., a + rope + 2 * conv_ch:]
    half = rope // 2
    w_sw = jnp.concatenate([-w_pe[..., half:], w_pe[..., :half]], axis=-1)
    zpad = jnp.zeros(w_pe.shape[:-1] + (LANES - rope,), w_in.dtype)
    return jnp.concatenate([w_q, w_kv, w_glu, w_pool, w_pe, zpad, w_sw, zpad], axis=-1).astype(BF16)


def _prep_q_weights(w_q_b, dm):
    depth, q_rank, heads, _ = w_q_b.shape
    nope, rope = dm['nope'], dm['rope']
    half = rope // 2
    w_nope = w_q_b[..., :nope].reshape(depth, q_rank, heads * nope)
    w_pe = w_q_b[..., nope:]
    w_sw = jnp.concatenate([-w_pe[..., half:], w_pe[..., :half]], axis=-1)
    zpad = jnp.zeros(w_pe.shape[:-1] + (LANES - rope,), w_q_b.dtype)
    pad = lambda w: jnp.concatenate([w, zpad], axis=-1).reshape(depth, q_rank, heads * LANES)
    return jnp.concatenate([w_nope, pad(w_pe), pad(w_sw)], axis=-1).astype(BF16)


def _rope_tables(seq, dec_seq, dec_tokens, past, rope):
    half = rope // 2
    inv = ROPE_BASE ** (-jnp.arange(half, dtype=F32) / half)

    def tab(pos):
        ang = pos.astype(F32)[:, None] * inv[None, :]
        z = jnp.zeros((pos.shape[0], LANES - rope), F32)
        c, s = jnp.cos(ang), jnp.sin(ang)
        return jnp.concatenate([c, c, z], axis=1), jnp.concatenate([s, s, z], axis=1)

    cp, sp = tab(jnp.arange(seq, dtype=jnp.int32))
    cs, ss = tab(past + jnp.arange(dec_seq, dtype=jnp.int32))
    reps = dec_tokens // dec_seq
    return (jnp.concatenate([cp, jnp.tile(cs, (reps, 1))], axis=0),
            jnp.concatenate([sp, jnp.tile(ss, (reps, 1))], axis=0))


def kernel(x_prompt, x_sample, cache_ckv, cache_kpe, state_conv, state_pool, page_table, w_in, q_norm_g, kv_norm_g, w_q_b, w_uk, w_uv, conv_w, conv_b, conv_ln_g, conv_ln_b, pool_w, pool_scale, w_out, ln1_g, ln1_b, w_ffn_gate, w_ffn_up, w_ffn_down, w_router, b_router, w_exp_gate, w_exp_up, w_exp_down, ln2_g, ln2_b):
    batch, seq, d_model = x_prompt.shape
    dec_batch, dec_seq, _ = x_sample.shape
    depth = w_in.shape[0]
    q_rank, heads, qk = w_q_b.shape[1:]
    kv_rank, _, nope = w_uk.shape[1:]
    v_dim = w_uv.shape[3]
    rope = qk - nope
    conv_ch = conv_w.shape[2]
    pool_ch = pool_scale.shape[1]
    conv_past = conv_w.shape[1] - 1
    pool_past = state_pool.shape[2]
    n_groups, pool_gc = pool_w.shape[1], pool_w.shape[2]
    n_exp = w_router.shape[2]
    past = page_table.shape[1] * PAGE_SIZE
    n_p, n_s = batch * seq, dec_batch * dec_seq
    tm = TOKEN_TILE
    assert n_p % tm == 0 and n_s % tm == 0 and seq % tm == 0
    assert cache_ckv.shape[2] == PAGE_SIZE and n_groups == len(POOL_WINDOWS)
    assert dec_batch % DECODE_GROUP == 0 and n_p % (DECODE_GROUP * dec_seq) == 0
    assert dec_seq & (dec_seq - 1) == 0 and ATTN_TILE & (ATTN_TILE - 1) == 0
    dm = dict(heads=heads, q_rank=q_rank, kv_rank=kv_rank, nope=nope, rope=rope, v_dim=v_dim,
              conv_ch=conv_ch, pool_ch=pool_ch, q_scale=float(qk) ** -0.5 * LOG2_E,
              alpha=float(2 * depth) ** 0.25)

    win_all = _prep_in_weights(w_in, dm)
    wq_all = _prep_q_weights(w_q_b, dm)
    wuk_all = jnp.transpose(w_uk, (0, 2, 3, 1)).astype(BF16)
    wkn_all = w_uk.reshape(depth, kv_rank, heads * nope).astype(BF16)
    wuv_all = jnp.transpose(w_uv, (0, 2, 1, 3)).astype(BF16)
    eye = jnp.eye(n_groups, dtype=pool_w.dtype)
    pw_all = jnp.einsum('lgcd,gh->lgchd', pool_w, eye).reshape(depth, pool_ch, pool_ch).astype(BF16)
    wout_all = w_out.astype(BF16)
    wr_all = jnp.pad(w_router, ((0, 0), (0, 0), (0, LANES - n_exp)))
    wrh_all = wr_all.astype(BF16)
    wrl_all = (wr_all - wrh_all.astype(F32)).astype(BF16)
    br_all = jnp.pad(b_router, ((0, 0), (0, LANES - n_exp)), constant_values=NEG_BIG)[:, None, :]
    cache_kpe_t = jnp.swapaxes(cache_kpe, 2, 3)
    vec = lambda a, l: a[l][None, :]

    cos_tab, sin_tab = _rope_tables(seq, dec_seq, n_s, past, rope)
    tiles_per_seq, prompt_tiles = seq // tm, n_p // tm
    tab_index = lambda i: jnp.where(i < prompt_tiles, lax.rem(i, tiles_per_seq),
                                    tiles_per_seq + i - prompt_tiles)

    weg, weu, wed = w_exp_gate.astype(BF16), w_exp_up.astype(BF16), w_exp_down.astype(BF16)
    xp, xs = x_prompt.reshape(n_p, d_model), x_sample.reshape(n_s, d_model)
    outs = [[] for _ in range(8)]
    for l in range(depth):
        qh, kh, qcat_s, kcat, ckv_p, kpe_p, ckv_s, kpe_s, u, hp = _in_proj(
            xp, xs, win_all[l], wq_all[l], wuk_all[l], wkn_all[l], vec(q_norm_g, l),
            vec(kv_norm_g, l), cos_tab, sin_tab, tab_index, n_p, dm)

        olat_p = _prompt_attn(qh, kh, kcat, batch, seq, dm)
        olat_s = _decode_attn(page_table, qcat_s, kcat, cache_ckv, cache_kpe_t, l,
                              n_p, dec_batch, dec_seq, dm)

        seq_w = (conv_w[l], vec(conv_b, l), vec(conv_ln_g, l), vec(conv_ln_b, l),
                 pw_all[l], vec(pool_scale, l))
        ocp_p, ctail, ptail = _seq_prompt(u, hp, *seq_w, batch, seq, dm)
        cext = jnp.concatenate([state_conv[l], u[n_p:].reshape(dec_batch, dec_seq, conv_ch)], axis=1)
        pext = jnp.concatenate([state_pool[l], hp[n_p:].reshape(dec_batch, dec_seq, pool_ch)], axis=1)
        ocp_s = _seq_sample(jnp.swapaxes(cext, 0, 1), jnp.swapaxes(pext, 0, 1), *seq_w,
                            dec_seq, past, dm)
        ocp_s = jnp.swapaxes(ocp_s, 0, 1).reshape(n_s, conv_ch + pool_ch)

        i = l // 2
        router = (wrh_all[i], wrl_all[i], br_all[i]) if l % 2 == 1 else None
        res = _out_proj(olat_p, olat_s, ocp_p, ocp_s, xp, xs, wuv_all[l], wout_all[l],
                        vec(ln1_g, l), vec(ln1_b, l), router, n_p, dm)
        last = l == depth - 1
        if l % 2 == 0:
            x1, x1b = res
            wg, wu, wd = (w[i].astype(BF16) for w in (w_ffn_gate, w_ffn_up, w_ffn_down))
            x = _ffn_dense(x1b, x1, wg, wu, wd, vec(ln2_g, l), vec(ln2_b, l), n_p, last, dm)
        else:
            x1, x1b, route = res
            x = _moe_layer(x1, x1b, route, weg, weu, wed, i, vec(ln2_g, l), vec(ln2_b, l),
                           n_p, last, dm)
        if not last:
            xp, xs = x, x[n_p:]

        outs[0].append(ckv_p.reshape(batch, seq, kv_rank))
        outs[1].append(kpe_p.reshape(batch, seq, rope))
        outs[2].append(ctail[:, ctail.shape[1] - conv_past:])
        outs[3].append(ptail[:, ptail.shape[1] - pool_past:])
        outs[4].append(ckv_s.reshape(dec_batch, dec_seq, kv_rank))
        outs[5].append(kpe_s.reshape(dec_batch, dec_seq, rope))
        outs[6].append(cext[:, dec_seq:])
        outs[7].append(pext[:, dec_seq:])

    y_p, y_s = x
    return (y_p.reshape(batch, seq, d_model), y_s.reshape(dec_batch, dec_seq, d_model),
            *[jnp.stack(o) for o in outs])
```

```python
import functools

import jax
import jax.numpy as jnp
from jax import lax
from jax.experimental import pallas as pl
from jax.experimental.pallas import tpu as pltpu

F32 = jnp.float32
BF16 = jnp.bfloat16

LN_EPS = 1e-5
RMS_EPS = 1e-6
ROPE_BASE = 10000.0
PAGE_SIZE = 128
TOP_K = 2
POOL_WINDOWS = (2, 4, 8, 16)

LANES = 128
SUBLANES = 8
V7X_VMEM_LIMIT = 56 * 1024 * 1024
TOKEN_TILE = 512
ATTN_TILE = 512
SEQ_TILE = 256
CONV_CHUNK = 64
DECODE_GROUP = 4
DECODE_CHUNK = 2048
MXU_TILE = 256
FF_CHUNK = 4 * MXU_TILE
NEG_BIG = -1e30
LOG2_E = 1.4426950408889634


def _cparams(sem):
    return pltpu.CompilerParams(dimension_semantics=sem, vmem_limit_bytes=V7X_VMEM_LIMIT)


def _dot(a, b):
    return jnp.dot(a, b, preferred_element_type=F32)


def _dot_nt(a, b):
    return lax.dot_general(a, b, (((1,), (1,)), ((), ())), preferred_element_type=F32)


def _lane_repeat(x, n):
    return x if n == 1 else jnp.concatenate([x] * n, axis=1)


def _sigmoid(x):
    return 1.0 / (1.0 + jnp.exp(-x))


def _layer_norm(y, g, b):
    mu = jnp.mean(y, axis=-1, keepdims=True)
    yc = y - mu
    var = jnp.mean(yc * yc, axis=-1, keepdims=True)
    return yc * lax.rsqrt(var + LN_EPS) * g + b


def _rms_norm(h, g):
    return h * lax.rsqrt(jnp.mean(h * h, axis=-1, keepdims=True) + RMS_EPS) * g


def _const_spec(shape):
    nd = len(shape)
    return pl.BlockSpec(shape, lambda *_: (0,) * nd)


def _resident_spec(shape):
    nd = len(shape)
    return pl.BlockSpec(shape, lambda *_: (0,) * nd, pipeline_mode=pl.Buffered(1))


def _pick_rows(prompt_ref, sample_ref, prompt_tiles):
    return jnp.where(pl.program_id(0) < prompt_tiles, prompt_ref[...], sample_ref[...])


def _in_proj_kernel(xp_ref, xs_ref, win_ref, wq_ref, wuk_ref, wkn_ref, qg_ref, kvg_ref, cos_ref, sin_ref,
                    qh_ref, kh_ref, qcat_ref, kcat_ref, ckvp_ref, kpep_ref, ckvs_ref, kpes_ref,
                    u_ref, hp_ref,
                    *, heads, q_rank, kv_rank, conv_ch, pool_ch, nope, rope, scale, prompt_tiles):
    x = _pick_rows(xp_ref, xs_ref, prompt_tiles).astype(BF16)
    h = _dot(x, win_ref[...])
    o_kv = q_rank
    o_glu = o_kv + kv_rank
    o_pool = o_glu + 2 * conv_ch
    o_pe = o_pool + pool_ch
    o_sw = o_pe + LANES
    cos = cos_ref[...]
    sin = sin_ref[...]

    qn = _rms_norm(h[:, :q_rank], qg_ref[...]).astype(BF16)
    q = _dot(qn, wq_ref[...])
    o_qpe = heads * nope
    o_qsw = o_qpe + heads * LANES
    ckv = _rms_norm(h[:, o_kv:o_kv + kv_rank], kvg_ref[...])
    kpe = h[:, o_pe:o_pe + LANES] * cos + h[:, o_sw:o_sw + LANES] * sin
    ckv_b = ckv.astype(BF16)
    kpe_b = kpe.astype(BF16)
    kcat_ref[:, 0:kv_rank] = ckv_b
    kcat_ref[:, kv_rank:] = kpe_b
    k_nope = _dot(ckv_b, wkn_ref[...])

    q_pe = []
    for hd in range(heads):
        pe = q[:, o_qpe + hd * LANES:o_qpe + (hd + 1) * LANES]
        sw = q[:, o_qsw + hd * LANES:o_qsw + (hd + 1) * LANES]
        q_pe.append(((pe * cos + sw * sin) * scale).astype(BF16))
        qh_ref[hd, :, 0:nope] = (q[:, hd * nope:(hd + 1) * nope] * scale).astype(BF16)
        qh_ref[hd, :, nope:] = q_pe[hd]
        kh_ref[hd, :, 0:nope] = k_nope[:, hd * nope:(hd + 1) * nope].astype(BF16)
        kh_ref[hd, :, nope:] = kpe_b

    for hd in range(heads):
        q_nope = q[:, hd * nope:(hd + 1) * nope].astype(BF16)
        qcat_ref[hd, :, 0:kv_rank] = (_dot(q_nope, wuk_ref[hd]) * scale).astype(BF16)
        qcat_ref[hd, :, kv_rank:] = q_pe[hd]

    @pl.when(pl.program_id(0) < prompt_tiles)
    def _():
        ckvp_ref[...] = ckv
        kpep_ref[...] = kpe[:, :rope]

    @pl.when(pl.program_id(0) >= prompt_tiles)
    def _():
        ckvs_ref[...] = ckv
        kpes_ref[...] = kpe[:, :rope]

    u_ref[...] = h[:, o_glu:o_glu + conv_ch] * _sigmoid(h[:, o_glu + conv_ch:o_glu + 2 * conv_ch])
    hp_ref[...] = h[:, o_pool:o_pool + pool_ch]


def _in_proj(xp, xs, win, wq, wuk, wkn, qg, kvg, cos_tab, sin_tab, tab_index, n_prompt, dm):
    d = xp.shape[1]
    n = n_prompt + xs.shape[0]
    tm = TOKEN_TILE
    heads, q_rank, kv_rank = dm['heads'], dm['q_rank'], dm['kv_rank']
    conv_ch, pool_ch, rope, nope = dm['conv_ch'], dm['pool_ch'], dm['rope'], dm['nope']
    prompt_tiles = n_prompt // tm
    row = lambda i: (i, 0)
    hrow = lambda i: (0, i, 0)
    prow = lambda i: (jnp.minimum(i, prompt_tiles - 1), 0)
    srow = lambda i: (jnp.maximum(i - prompt_tiles, 0), 0)
    hsrow = lambda i: (0, jnp.maximum(i - prompt_tiles, 0), 0)
    kern = functools.partial(_in_proj_kernel, heads=heads, q_rank=q_rank, kv_rank=kv_rank,
                             conv_ch=conv_ch, pool_ch=pool_ch, nope=dm['nope'], rope=rope,
                             scale=dm['q_scale'], prompt_tiles=prompt_tiles)
    return pl.pallas_call(
        kern,
        grid=(n // tm,),
        in_specs=[
            pl.BlockSpec((tm, d), prow), pl.BlockSpec((tm, d), srow),
            _const_spec(win.shape), _const_spec(wq.shape), _const_spec(wuk.shape),
            _const_spec(wkn.shape), _const_spec(qg.shape), _const_spec(kvg.shape),
            pl.BlockSpec((tm, LANES), lambda i: (tab_index(i), 0)),
            pl.BlockSpec((tm, LANES), lambda i: (tab_index(i), 0)),
        ],
        out_specs=[
            pl.BlockSpec((heads, tm, nope + LANES), hrow),
            pl.BlockSpec((heads, tm, nope + LANES), hrow),
            pl.BlockSpec((heads, tm, kv_rank + LANES), hsrow),
            pl.BlockSpec((tm, kv_rank + LANES), row),
            pl.BlockSpec((tm, kv_rank), prow),
            pl.BlockSpec((tm, rope), prow),
            pl.BlockSpec((tm, kv_rank), srow),
            pl.BlockSpec((tm, rope), srow),
            pl.BlockSpec((tm, conv_ch), row),
            pl.BlockSpec((tm, pool_ch), row),
        ],
        out_shape=[
            jax.ShapeDtypeStruct((heads, n, nope + LANES), BF16),
            jax.ShapeDtypeStruct((heads, n, nope + LANES), BF16),
            jax.ShapeDtypeStruct((heads, n - n_prompt, kv_rank + LANES), BF16),
            jax.ShapeDtypeStruct((n, kv_rank + LANES), BF16),
            jax.ShapeDtypeStruct((n_prompt, kv_rank), F32),
            jax.ShapeDtypeStruct((n_prompt, rope), F32),
            jax.ShapeDtypeStruct((n - n_prompt, kv_rank), F32),
            jax.ShapeDtypeStruct((n - n_prompt, rope), F32),
            jax.ShapeDtypeStruct((n, conv_ch), F32),
            jax.ShapeDtypeStruct((n, pool_ch), F32),
        ],
        compiler_params=_cparams(("arbitrary",)),
        name="in_proj",
    )(xp, xs, win, wq, wuk, wkn, qg, kvg, cos_tab, sin_tab)


def _prompt_attn_kernel(q_ref, kh_ref, kc_ref, o_ref, m_sc, l_sc, acc_sc, *, heads, tile, kv_rank):
    qi = pl.program_id(1)
    m_sc[...] = jnp.full(m_sc.shape, -jnp.inf, F32)
    l_sc[...] = jnp.zeros(l_sc.shape, F32)
    acc_sc[...] = jnp.zeros(acc_sc.shape, F32)

    def step(ki, masked):
        start = pl.multiple_of(ki * tile, tile)
        kc = kc_ref[pl.ds(start, tile), 0:kv_rank]
        for hd in range(heads):
            rows = pl.ds(hd * tile, tile)
            s = _dot_nt(q_ref[hd], kh_ref[hd, pl.ds(start, tile), :])
            if masked:
                r = lax.broadcasted_iota(jnp.int32, s.shape, 0)
                c = lax.broadcasted_iota(jnp.int32, s.shape, 1)
                s = jnp.where(c <= r, s, -jnp.inf)
            m_prev = m_sc[rows, :]
            m_new = jnp.maximum(m_prev, jnp.max(s, axis=1, keepdims=True))
            p = jnp.exp2(s - _lane_repeat(m_new, tile // LANES))
            alpha = jnp.exp2(m_prev - m_new)
            l_sc[rows, :] = alpha * l_sc[rows, :] + jnp.sum(p, axis=1, keepdims=True)
            acc_sc[rows, :] = (acc_sc[rows, :] * _lane_repeat(alpha, kv_rank // LANES)
                               + _dot(p.astype(BF16), kc))
            m_sc[rows, :] = m_new

    def body(ki, carry):
        step(ki, False)
        return carry

    lax.fori_loop(0, qi, body, 0)
    step(qi, True)
    inv = 1.0 / l_sc[...]
    o = acc_sc[...] * _lane_repeat(inv, kv_rank // LANES)
    o_ref[...] = o.reshape(heads, tile, kv_rank).astype(BF16)


def _prompt_attn(qh, kh, kcat, batch, seq, dm):
    heads, kv_rank = dm['heads'], dm['kv_rank']
    width = qh.shape[2]
    tile = min(ATTN_TILE, seq)
    nq = seq // tile
    rows = heads * tile
    kern = functools.partial(_prompt_attn_kernel, heads=heads, tile=tile, kv_rank=kv_rank)
    qblk = lambda b, i: (0, b * nq + i, 0)
    return pl.pallas_call(
        kern,
        grid=(batch, nq),
        in_specs=[
            pl.BlockSpec((heads, tile, width), qblk),
            pl.BlockSpec((heads, seq, width), lambda b, i: (0, b, 0)),
            pl.BlockSpec((seq, kcat.shape[1]), lambda b, i: (b, 0)),
        ],
        out_specs=pl.BlockSpec((heads, tile, kv_rank), qblk),
        out_shape=jax.ShapeDtypeStruct((heads, batch * seq, kv_rank), BF16),
        scratch_shapes=[
            pltpu.VMEM((rows, LANES), F32),
            pltpu.VMEM((rows, LANES), F32),
            pltpu.VMEM((rows, kv_rank), F32),
        ],
        compiler_params=_cparams(("parallel", "arbitrary")),
        name="prompt_attn",
    )(qh, kh, kcat)


def _decode_attn_kernel(pt_ref, q_ref, kn_ref, cache_c, cache_p,
                        o_ref, kbuf, pbuf, sem,
                        *, layer, heads, group, dec_seq, n_pages, kv_rank, rope):
    g = pl.program_id(0)
    n_seq = pl.num_programs(0)
    slot = lax.rem(g, 2)
    j = lax.rem(g, group)
    rows_h = group * dec_seq
    rows = heads * rows_h

    def start_seq(seq, sl):
        def body(p, c):
            pg = pt_ref[seq * n_pages + p]
            dst = pl.ds(pl.multiple_of(p * PAGE_SIZE, PAGE_SIZE), PAGE_SIZE)
            pltpu.make_async_copy(cache_c.at[layer, pg], kbuf.at[sl, dst], sem.at[0, sl]).start()
            pltpu.make_async_copy(cache_p.at[layer, pg], pbuf.at[sl, :, dst], sem.at[1, sl]).start()
            return c
        lax.fori_loop(0, n_pages, body, 0, unroll=min(8, n_pages))

    def wait_seq(sl):
        pltpu.make_async_copy(kbuf.at[sl], kbuf.at[sl], sem.at[0, sl]).wait()
        pltpu.make_async_copy(pbuf.at[sl], pbuf.at[sl], sem.at[1, sl]).wait()

    @pl.when(g == 0)
    def _():
        start_seq(g, slot)

    @pl.when(g + 1 < n_seq)
    def _():
        start_seq(g + 1, 1 - slot)

    wait_seq(slot)

    q = q_ref[...].reshape(rows, q_ref.shape[2])
    q_lat = q[:, :kv_rank]
    q_pe = q[:, kv_rank:kv_rank + rope]

    kn = kn_ref[...]
    s_new = _dot_nt(q, kn)
    r = lax.broadcasted_iota(jnp.int32, s_new.shape, 0)
    c = lax.broadcasted_iota(jnp.int32, s_new.shape, 1)
    shift = dec_seq.bit_length() - 1
    visible = ((c >> shift) == j) & ((c & (dec_seq - 1)) <= (r & (dec_seq - 1)))
    s_new = jnp.where(visible, s_new, -jnp.inf)
    m = jnp.max(s_new, axis=1, keepdims=True)
    p_new = jnp.exp2(s_new - m)
    denom = jnp.sum(p_new, axis=1, keepdims=True)
    o = _dot(p_new.astype(BF16), kn[:, :kv_rank])

    past = kbuf.shape[1]
    chunk = min(DECODE_CHUNK, past)
    keys, scores = [], []
    for c0 in range(0, past, chunk):
        kb = kbuf[slot, c0:c0 + chunk, :].astype(BF16)
        pb = pbuf[slot, :, c0:c0 + chunk].astype(BF16)
        keys.append(kb)
        scores.append(_dot_nt(q_lat, kb) + _dot(q_pe, pb))
    for kb, s in zip(keys, scores):
        m_next = jnp.maximum(m, jnp.max(s, axis=1, keepdims=True))
        alpha = jnp.exp2(m - m_next)
        p = jnp.exp2(s - m_next)
        denom = alpha * denom + jnp.sum(p, axis=1, keepdims=True)
        o = alpha * o + _dot(p.astype(BF16), kb)
        m = m_next
    o = (o / denom).reshape(heads, rows_h, kv_rank).astype(BF16)

    @pl.when(j == 0)
    def _():
        o_ref[...] = o

    @pl.when(j != 0)
    def _():
        rr = lax.broadcasted_iota(jnp.int32, o.shape, 1)
        o_ref[...] = jnp.where((rr >> shift) == j, o, o_ref[...])


def _decode_attn(page_table, qcat, kcat, cache_ckv, cache_kpe_t, layer,
                 n_prompt, dec_batch, dec_seq, dm):
    heads, kv_rank, rope = dm['heads'], dm['kv_rank'], dm['rope']
    width = qcat.shape[2]
    group = DECODE_GROUP
    rows_h = group * dec_seq
    n_pages = page_table.shape[1]
    past = n_pages * PAGE_SIZE
    base = n_prompt // rows_h
    hrow = lambda g, pt: (0, g // group, 0)
    row = lambda g, pt: (base + g // group, 0)
    kern = functools.partial(_decode_attn_kernel, layer=layer, heads=heads, group=group,
                             dec_seq=dec_seq, n_pages=n_pages, kv_rank=kv_rank, rope=rope)
    grid_spec = pltpu.PrefetchScalarGridSpec(
        num_scalar_prefetch=1,
        grid=(dec_batch,),
        in_specs=[
            pl.BlockSpec((heads, rows_h, width), hrow),
            pl.BlockSpec((rows_h, width), row),
            pl.BlockSpec(memory_space=pl.ANY),
            pl.BlockSpec(memory_space=pl.ANY),
        ],
        out_specs=pl.BlockSpec((heads, rows_h, kv_rank), hrow),
        scratch_shapes=[
            pltpu.VMEM((2, past, kv_rank), F32),
            pltpu.VMEM((2, rope, past), F32),
            pltpu.SemaphoreType.DMA((2, 2)),
        ],
    )
    return pl.pallas_call(
        kern,
        grid_spec=grid_spec,
        out_shape=jax.ShapeDtypeStruct((heads, dec_batch * dec_seq, kv_rank), BF16),
        compiler_params=_cparams(("arbitrary",)),
        name="decode_attn",
    )(page_table.reshape(-1), qcat, kcat, cache_ckv, cache_kpe_t)


def _pool_lane_select(shape, vals):
    lane_group = lax.broadcasted_iota(jnp.int32, shape, len(shape) - 1) // (shape[-1] // len(vals))
    out = vals[-1]
    for gi in range(len(vals) - 2, -1, -1):
        out = jnp.where(lane_group == gi, vals[gi], out)
    return out


def _seq_prompt_kernel(u_ref, hp_ref, cw8_ref, cb_ref, cg_ref, cbeta_ref, pw_ref, ps_ref,
                       o_ref, ctail_ref, ptail_ref, cext, pext, shifted,
                       *, tile, conv_w, conv_ch, pool_ch):
    j = pl.program_id(1)
    chalo = cext.shape[0] - tile
    phalo = pext.shape[0] - tile

    @pl.when(j == 0)
    def _():
        cext[0:chalo, :] = jnp.zeros((chalo, conv_ch), F32)
        pext[0:phalo, :] = jnp.zeros((phalo, pool_ch), F32)

    cext[chalo:, :] = u_ref[...]
    pext[phalo:, :] = hp_ref[...]

    span = cext.shape[0] - SUBLANES
    for b in range(1, SUBLANES):
        shifted[b, 0:span, :] = cext[pl.ds(b, span), :]
    lead = chalo - (conv_w - 1)
    for c0 in range(0, tile, CONV_CHUNK):
        acc = None
        for k in range(conv_w):
            a, b = divmod(lead + k, SUBLANES)
            win = pl.ds(c0 + a * SUBLANES, CONV_CHUNK)
            rows = cext[win, :] if b == 0 else shifted[b, win, :]
            term = rows.reshape(CONV_CHUNK // SUBLANES, SUBLANES, conv_ch) * cw8_ref[k][None]
            acc = term if acc is None else acc + term
        acc = acc.reshape(CONV_CHUNK, conv_ch)
        y = _layer_norm(acc + cb_ref[...], cg_ref[...], cbeta_ref[...])
        o_ref[c0:c0 + CONV_CHUNK, 0:conv_ch] = (y * _sigmoid(y)).astype(BF16)

    xe = pext[...]
    s2 = xe + pltpu.roll(xe, 1, 0)
    s4 = s2 + pltpu.roll(s2, 2, 0)
    s8 = s4 + pltpu.roll(s4, 4, 0)
    s16 = s8 + pltpu.roll(s8, 8, 0)
    sums = [s[phalo:, :] for s in (s2, s4, s8, s16)]
    v = xe[phalo:, :]
    shape = v.shape
    pos = j * tile + lax.broadcasted_iota(jnp.int32, shape, 0)
    win = _pool_lane_select(shape, [jnp.full(shape, w, jnp.int32) for w in POOL_WINDOWS])
    cnt = jnp.minimum(pos + 1, win).astype(F32)
    mean = _pool_lane_select(shape, sums) / cnt
    d = (mean - v).astype(BF16)
    o_ref[:, conv_ch:conv_ch + pool_ch] = (_dot(d, pw_ref[...]) * ps_ref[...]).astype(BF16)

    cext[0:chalo, :] = cext[tile:tile + chalo, :]
    pext[0:phalo, :] = pext[tile:tile + phalo, :]

    @pl.when(j == pl.num_programs(1) - 1)
    def _():
        ctail_ref[0] = cext[tile:tile + chalo, :]
        ptail_ref[0] = pext[tile:tile + phalo, :]


def _seq_prompt(u, hp, cw, cb, cg, cbeta, pw, ps, batch, seq, dm):
    conv_ch, pool_ch = dm['conv_ch'], dm['pool_ch']
    conv_w = cw.shape[0]
    cw = jnp.broadcast_to(cw[:, None, :], (conv_w, SUBLANES, conv_ch))
    tile = min(SEQ_TILE, seq)
    nt = seq // tile
    chalo = -(-(conv_w - 1) // 8) * 8
    phalo = -(-max(POOL_WINDOWS) // 8) * 8
    kern = functools.partial(_seq_prompt_kernel, tile=tile, conv_w=conv_w,
                             conv_ch=conv_ch, pool_ch=pool_ch)
    row = lambda b, j: (b * nt + j, 0)
    tail = lambda b, j: (b, 0, 0)
    return pl.pallas_call(
        kern,
        grid=(batch, nt),
        in_specs=[
            pl.BlockSpec((tile, conv_ch), row),
            pl.BlockSpec((tile, pool_ch), row),
            _const_spec(cw.shape), _const_spec(cb.shape), _const_spec(cg.shape),
            _const_spec(cbeta.shape), _const_spec(pw.shape), _const_spec(ps.shape),
        ],
        out_specs=[pl.BlockSpec((tile, conv_ch + pool_ch), row),
                   pl.BlockSpec((1, chalo, conv_ch), tail),
                   pl.BlockSpec((1, phalo, pool_ch), tail)],
        out_shape=[jax.ShapeDtypeStruct((batch * seq, conv_ch + pool_ch), BF16),
                   jax.ShapeDtypeStruct((batch, chalo, conv_ch), F32),
                   jax.ShapeDtypeStruct((batch, phalo, pool_ch), F32)],
        scratch_shapes=[
            pltpu.VMEM((chalo + tile, conv_ch), F32),
            pltpu.VMEM((phalo + tile, pool_ch), F32),
            pltpu.VMEM((SUBLANES, chalo + tile - SUBLANES, conv_ch), F32),
        ],
        compiler_params=_cparams(("parallel", "arbitrary")),
        name="seq_prompt",
    )(u, hp, cw, cb, cg, cbeta, pw, ps)


def _seq_sample_kernel(cext_ref, pext_ref, cw_ref, cb_ref, cg_ref, cbeta_ref, pw_ref, ps_ref,
                       o_ref, *, dec_seq, conv_w, conv_ch, pool_ch, pos0):
    hist = pext_ref.shape[0] - dec_seq
    for t in range(dec_seq):
        acc = None
        for k in range(conv_w):
            term = cext_ref[t + k] * cw_ref[k:k + 1, :]
            acc = term if acc is None else acc + term
        y = _layer_norm(acc + cb_ref[...], cg_ref[...], cbeta_ref[...])
        o_ref[t, :, 0:conv_ch] = (y * _sigmoid(y)).astype(BF16)

        v = pext_ref[hist + t]
        sums = []
        run = None
        i = 0
        for w in POOL_WINDOWS:
            while i < w:
                row = pext_ref[hist + t - i]
                run = row if run is None else run + row
                i += 1
            sums.append(run / float(min(pos0 + t + 1, w)))
        mean = _pool_lane_select(v.shape, sums)
        d = (mean - v).astype(BF16)
        o_ref[t, :, conv_ch:conv_ch + pool_ch] = (_dot(d, pw_ref[...]) * ps_ref[...]).astype(BF16)


def _seq_sample(cext, pext, cw, cb, cg, cbeta, pw, ps, dec_seq, pos0, dm):
    conv_ch, pool_ch = dm['conv_ch'], dm['pool_ch']
    db = cext.shape[1]
    kern = functools.partial(_seq_sample_kernel, dec_seq=dec_seq, conv_w=cw.shape[0],
                             conv_ch=conv_ch, pool_ch=pool_ch, pos0=pos0)
    args = (cext, pext, cw, cb, cg, cbeta, pw, ps)
    return pl.pallas_call(
        kern,
        grid=(1,),
        in_specs=[_const_spec(a.shape) for a in args],
        out_specs=_const_spec((dec_seq, db, conv_ch + pool_ch)),
        out_shape=jax.ShapeDtypeStruct((dec_seq, db, conv_ch + pool_ch), BF16),
        compiler_params=_cparams(("arbitrary",)),
        name="seq_sample",
    )(*args)


def _out_proj_kernel(*refs, heads, att_width, alpha, route, prompt_tiles):
    if route:
        (olp_ref, ols_ref, ocpp_ref, ocps_ref, xp_ref, xs_ref, wuv_ref, wout_ref, g_ref, b_ref,
         wrh_ref, wrl_ref, br_ref, x1_ref, x1b_ref, route_ref) = refs
    else:
        (olp_ref, ols_ref, ocpp_ref, ocps_ref, xp_ref, xs_ref, wuv_ref, wout_ref, g_ref, b_ref,
         x1_ref, x1b_ref) = refs
    olat = _pick_rows(olp_ref, ols_ref, prompt_tiles)
    parts = [_dot(olat[hd], wuv_ref[hd]) for hd in range(heads)]
    o_att = jnp.concatenate(parts, axis=1).astype(BF16)
    mix = (_dot(o_att, wout_ref[0:att_width, :])
           + _dot(_pick_rows(ocpp_ref, ocps_ref, prompt_tiles), wout_ref[att_width:, :]))
    x1 = _layer_norm(alpha * _pick_rows(xp_ref, xs_ref, prompt_tiles) + mix, g_ref[...], b_ref[...])
    x1_ref[...] = x1
    x1h = x1.astype(BF16)
    x1b_ref[...] = x1h
    if route:
        x1l = (x1 - x1h.astype(F32)).astype(BF16)
        logits = (_dot(x1h, wrh_ref[...]) + _dot(x1l, wrh_ref[...]) + _dot(x1h, wrl_ref[...])
                  + br_ref[...])
        lane = lax.broadcasted_iota(jnp.int32, logits.shape, 1)
        m1 = jnp.max(logits, axis=1, keepdims=True)
        i1 = jnp.min(jnp.where(logits == m1, lane, LANES), axis=1, keepdims=True)
        rest = jnp.where(lane == i1, -jnp.inf, logits)
        m2 = jnp.max(rest, axis=1, keepdims=True)
        i2 = jnp.min(jnp.where(rest == m2, lane, LANES), axis=1, keepdims=True)
        e2 = jnp.exp(m2 - m1)
        g1 = 1.0 / (1.0 + e2)
        g2 = e2 / (1.0 + e2)
        packed = jnp.where(lane == 0, i1.astype(F32),
                           jnp.where(lane == 1, i2.astype(F32),
                                     jnp.where(lane == 2, g1, jnp.where(lane == 3, g2, 0.0))))
        route_ref[...] = packed[:, :route_ref.shape[1]]


def _out_proj(olat_p, olat_s, ocp_p, ocp_s, xp, xs, wuv, wout, g, b, router, n_prompt, dm):
    d = xp.shape[1]
    n = n_prompt + xs.shape[0]
    tm = TOKEN_TILE
    pt = n_prompt // tm
    heads, kv_rank = dm['heads'], dm['kv_rank']
    row = lambda i: (i, 0)
    prow = lambda i: (jnp.minimum(i, pt - 1), 0)
    srow = lambda i: (jnp.maximum(i - pt, 0), 0)
    width = ocp_p.shape[1]
    args = [olat_p, olat_s, ocp_p, ocp_s, xp, xs, wuv, wout, g, b]
    in_specs = [
        pl.BlockSpec((heads, tm, kv_rank), lambda i: (0, jnp.minimum(i, pt - 1), 0)),
        pl.BlockSpec((heads, tm, kv_rank), lambda i: (0, jnp.maximum(i - pt, 0), 0)),
        pl.BlockSpec((tm, width), prow), pl.BlockSpec((tm, width), srow),
        pl.BlockSpec((tm, d), prow), pl.BlockSpec((tm, d), srow),
        _const_spec(wuv.shape), _const_spec(wout.shape), _const_spec(g.shape), _const_spec(b.shape),
    ]
    out_specs = [pl.BlockSpec((tm, d), row), pl.BlockSpec((tm, d), row)]
    out_shape = [jax.ShapeDtypeStruct((n, d), F32), jax.ShapeDtypeStruct((n, d), BF16)]
    if router is not None:
        wrh, wrl, br = router
        args += [wrh, wrl, br]
        in_specs += [_const_spec(wrh.shape), _const_spec(wrl.shape), _const_spec(br.shape)]
        out_specs.append(pl.BlockSpec((tm, 8), row))
        out_shape.append(jax.ShapeDtypeStruct((n, 8), F32))
    kern = functools.partial(_out_proj_kernel, heads=heads, att_width=heads * dm['v_dim'],
                             alpha=dm['alpha'], route=router is not None, prompt_tiles=pt)
    return pl.pallas_call(
        kern, grid=(n // tm,), in_specs=in_specs, out_specs=out_specs, out_shape=out_shape,
        compiler_params=_cparams(("parallel",)), name="out_proj",
    )(*args)


def _ff_chunks(d_ff):
    bounds, c0 = [], 0
    while c0 < d_ff:
        c1 = min(c0 + FF_CHUNK, d_ff)
        bounds.append((c0, c1))
        c0 = c1
    return bounds


def _swiglu_tile(xb, wg_ref, wu_ref, wd_ref, d_ff, lead=()):
    y = None
    for c0, c1 in _ff_chunks(d_ff):
        gate = _dot(xb, wg_ref[lead + (slice(None), slice(c0, c1))])
        up = _dot(xb, wu_ref[lead + (slice(None), slice(c0, c1))])
        act = (gate * _sigmoid(gate) * up).astype(BF16)
        part = _dot(act, wd_ref[lead + (slice(c0, c1), slice(None))])
        y = part if y is None else y + part
    return y


def _store_token_rows(y, o_refs, prompt_tiles):
    if len(o_refs) == 1:
        o_refs[0][...] = y
        return

    @pl.when(pl.program_id(0) < prompt_tiles)
    def _():
        o_refs[0][...] = y

    @pl.when(pl.program_id(0) >= prompt_tiles)
    def _():
        o_refs[1][...] = y


def _token_out(n, d, n_prompt, split):
    tm = TOKEN_TILE
    if not split:
        return pl.BlockSpec((tm, d), lambda i, *_: (i, 0)), jax.ShapeDtypeStruct((n, d), F32)
    pt = n_prompt // tm
    specs = [pl.BlockSpec((tm, d), lambda i, *_: (jnp.minimum(i, pt - 1), 0)),
             pl.BlockSpec((tm, d), lambda i, *_: (jnp.maximum(i - pt, 0), 0))]
    shapes = [jax.ShapeDtypeStruct((n_prompt, d), F32), jax.ShapeDtypeStruct((n - n_prompt, d), F32)]
    return specs, shapes


def _ffn_dense_kernel(xb_ref, x_ref, wg_ref, wu_ref, wd_ref, g_ref, b_ref, *o_refs,
                      d_ff, alpha, prompt_tiles):
    y = _swiglu_tile(xb_ref[...], wg_ref, wu_ref, wd_ref, d_ff)
    _store_token_rows(_layer_norm(alpha * x_ref[...] + y, g_ref[...], b_ref[...]), o_refs, prompt_tiles)


def _ffn_dense(x1b, x1, wg, wu, wd, g, b, n_prompt, split, dm):
    n, d = x1.shape
    tm = TOKEN_TILE
    row = lambda i: (i, 0)
    kern = functools.partial(_ffn_dense_kernel, d_ff=wg.shape[1], alpha=dm['alpha'],
                             prompt_tiles=n_prompt // tm)
    out_specs, out_shape = _token_out(n, d, n_prompt, split)
    return pl.pallas_call(
        kern,
        grid=(n // tm,),
        in_specs=[pl.BlockSpec((tm, d), row), pl.BlockSpec((tm, d), row),
                  _resident_spec(wg.shape), _resident_spec(wu.shape), _resident_spec(wd.shape),
                  _const_spec(g.shape), _const_spec(b.shape)],
        out_specs=out_specs,
        out_shape=out_shape,
        compiler_params=_cparams(("arbitrary",)),
        name="ffn_dense",
    )(x1b, x1, wg, wu, wd, g, b)


def _moe_experts_kernel(te_ref, nu_ref, xs_ref, wg_ref, wu_ref, wd_ref, o_ref, *, d_ff):
    del te_ref

    @pl.when(pl.program_id(0) < nu_ref[0])
    def _():
        o_ref[...] = _swiglu_tile(xs_ref[...], wg_ref, wu_ref, wd_ref, d_ff, lead=(0,)).astype(o_ref.dtype)

    @pl.when(pl.program_id(0) >= nu_ref[0])
    def _():
        o_ref[...] = jnp.zeros(o_ref.shape, o_ref.dtype)


def _moe_experts(tile_expert, n_used, xs, wg, wu, wd, layer):
    rows, d = xs.shape
    tm = TOKEN_TILE
    d_ff = wg.shape[3]
    wspec = lambda shape: pl.BlockSpec((None, 1) + shape[2:], lambda i, te, nu: (layer, te[i], 0, 0),
                                       pipeline_mode=pl.Buffered(1))
    grid_spec = pltpu.PrefetchScalarGridSpec(
        num_scalar_prefetch=2,
        grid=(rows // tm,),
        in_specs=[pl.BlockSpec((tm, d), lambda i, te, nu: (i, 0)),
                  wspec(wg.shape), wspec(wu.shape), wspec(wd.shape)],
        out_specs=pl.BlockSpec((tm, d), lambda i, te, nu: (i, 0)),
    )
    return pl.pallas_call(
        functools.partial(_moe_experts_kernel, d_ff=d_ff),
        grid_spec=grid_spec,
        out_shape=jax.ShapeDtypeStruct((rows, d), BF16),
        compiler_params=_cparams(("arbitrary",)),
        name="moe_experts",
    )(tile_expert, n_used, xs, wg, wu, wd)


def _moe_combine_kernel(x_ref, o1_ref, o2_ref, route_ref, g_ref, b_ref, *o_refs, alpha, prompt_tiles):
    route = route_ref[...]
    y = route[:, 2:3] * o1_ref[...].astype(F32) + route[:, 3:4] * o2_ref[...].astype(F32)
    _store_token_rows(_layer_norm(alpha * x_ref[...] + y, g_ref[...], b_ref[...]), o_refs, prompt_tiles)


def _moe_combine(x1, o1, o2, route, g, b, n_prompt, split, dm):
    n, d = x1.shape
    tm = TOKEN_TILE
    row = lambda i: (i, 0)
    out_specs, out_shape = _token_out(n, d, n_prompt, split)
    return pl.pallas_call(
        functools.partial(_moe_combine_kernel, alpha=dm['alpha'], prompt_tiles=n_prompt // tm),
        grid=(n // tm,),
        in_specs=[pl.BlockSpec((tm, d), row), pl.BlockSpec((tm, d), row), pl.BlockSpec((tm, d), row),
                  pl.BlockSpec((tm, route.shape[1]), row), _const_spec(g.shape), _const_spec(b.shape)],
        out_specs=out_specs,
        out_shape=out_shape,
        compiler_params=_cparams(("arbitrary",)),
        name="moe_combine",
    )(x1, o1, o2, route, g, b)


def _moe_layer(x1, x1b, route, wg, wu, wd, layer, g, b, n_prompt, split, dm):
    n = x1.shape[0]
    n_exp = wg.shape[1]
    tm = TOKEN_TILE
    n_assign = TOP_K * n
    n_tiles = n_assign // tm + n_exp
    e_flat = jnp.concatenate([route[:, 0], route[:, 1]]).astype(jnp.int32)
    onehot = (e_flat[:, None] == jnp.arange(n_exp, dtype=jnp.int32)[None, :]).astype(jnp.int32)
    cnt = jnp.sum(onehot, axis=0)
    rank = jnp.sum((jnp.cumsum(onehot, axis=0) - onehot) * onehot, axis=1)
    padded = ((cnt + tm - 1) // tm) * tm
    pad_end = jnp.cumsum(padded)
    pad_start = pad_end - padded
    cnt_start = jnp.cumsum(cnt) - cnt
    pos = pad_start[e_flat] + rank
    order = jnp.argsort(e_flat, stable=True).astype(jnp.int32)
    slot = jnp.arange(n_tiles * tm, dtype=jnp.int32)
    tile_expert = jnp.minimum(
        jnp.searchsorted(pad_end, jnp.arange(n_tiles, dtype=jnp.int32) * tm, side='right'),
        n_exp - 1).astype(jnp.int32)
    slot_e = jnp.repeat(tile_expert, tm)
    local = slot - pad_start[slot_e]
    src = jnp.where(local < cnt[slot_e], order[jnp.minimum(cnt_start[slot_e] + local, n_assign - 1)], 0)
    tok_sorted = jnp.where(src >= n, src - n, src)
    n_used = (pad_end[-1:] // tm).astype(jnp.int32)

    take = lambda a, idx: a.at[idx].get(mode='promise_in_bounds')
    xs = take(x1b, tok_sorted)
    out_sorted = _moe_experts(tile_expert, n_used, xs, wg, wu, wd, layer)
    return _moe_combine(x1, take(out_sorted, pos[:n]), take(out_sorted, pos[n:]), route, g, b,
                        n_prompt, split, dm)


def _prep_in_weights(w_in, dm):
    q_rank, kv_rank, rope, conv_ch, pool_ch = (dm['q_rank'], dm['kv_rank'], dm['rope'],
                                               dm['conv_ch'], dm['pool_ch'])
    a = q_rank + kv_rank
    w_q, w_kv = w_in[..., :q_rank], w_in[..., q_rank:a]
    w_pe = w_in[..., a:a + rope]
    w_glu = w_in[..., a + rope:a + rope + 2 * conv_ch]
    w_pool = w_in[..., a + rope + 2 * conv_ch:]
    half = rope // 2
    w_sw = jnp.concatenate([-w_pe[..., half:], w_pe[..., :half]], axis=-1)
    zpad = jnp.zeros(w_pe.shape[:-1] + (LANES - rope,), w_in.dtype)
    return jnp.concatenate([w_q, w_kv, w_glu, w_pool, w_pe, zpad, w_sw, zpad], axis=-1).astype(BF16)


def _prep_q_weights(w_q_b, dm):
    depth, q_rank, heads, _ = w_q_b.shape
    nope, rope = dm['nope'], dm['rope']
    half = rope // 2
    w_nope = w_q_b[..., :nope].reshape(depth, q_rank, heads * nope)
    w_pe = w_q_b[..., nope:]
    w_sw = jnp.concatenate([-w_pe[..., half:], w_pe[..., :half]], axis=-1)
    zpad = jnp.zeros(w_pe.shape[:-1] + (LANES - rope,), w_q_b.dtype)
    pad = lambda w: jnp.concatenate([w, zpad], axis=-1).reshape(depth, q_rank, heads * LANES)
    return jnp.concatenate([w_nope, pad(w_pe), pad(w_sw)], axis=-1).astype(BF16)


def _rope_tables(seq, dec_seq, dec_tokens, past, rope):
    half = rope // 2
    inv = ROPE_BASE ** (-jnp.arange(half, dtype=F32) / half)

    def tab(pos):
        ang = pos.astype(F32)[:, None] * inv[None, :]
        z = jnp.zeros((pos.shape[0], LANES - rope), F32)
        c, s = jnp.cos(ang), jnp.sin(ang)
        return jnp.concatenate([c, c, z], axis=1), jnp.concatenate([s, s, z], axis=1)

    cp, sp = tab(jnp.arange(seq, dtype=jnp.int32))
    cs, ss = tab(past + jnp.arange(dec_seq, dtype=jnp.int32))
    reps = dec_tokens // dec_seq
    return (jnp.concatenate([cp, jnp.tile(cs, (reps, 1))], axis=0),
            jnp.concatenate([sp, jnp.tile(ss, (reps, 1))], axis=0))


def kernel(x_prompt, x_sample, cache_ckv, cache_kpe, state_conv, state_pool, page_table, w_in, q_norm_g, kv_norm_g, w_q_b, w_uk, w_uv, conv_w, conv_b, conv_ln_g, conv_ln_b, pool_w, pool_scale, w_out, ln1_g, ln1_b, w_ffn_gate, w_ffn_up, w_ffn_down, w_router, b_router, w_exp_gate, w_exp_up, w_exp_down, ln2_g, ln2_b):
    batch, seq, d_model = x_prompt.shape
    dec_batch, dec_seq, _ = x_sample.shape
    depth = w_in.shape[0]
    q_rank, heads, qk = w_q_b.shape[1:]
    kv_rank, _, nope = w_uk.shape[1:]
    v_dim = w_uv.shape[3]
    rope = qk - nope
    conv_ch = conv_w.shape[2]
    pool_ch = pool_scale.shape[1]
    conv_past = conv_w.shape[1] - 1
    pool_past = state_pool.shape[2]
    n_groups, pool_gc = pool_w.shape[1], pool_w.shape[2]
    n_exp = w_router.shape[2]
    past = page_table.shape[1] * PAGE_SIZE
    n_p, n_s = batch * seq, dec_batch * dec_seq
    tm = TOKEN_TILE
    assert n_p % tm == 0 and n_s % tm == 0 and seq % tm == 0
    assert cache_ckv.shape[2] == PAGE_SIZE and n_groups == len(POOL_WINDOWS)
    assert dec_batch % DECODE_GROUP == 0 and n_p % (DECODE_GROUP * dec_seq) == 0
    assert dec_seq & (dec_seq - 1) == 0 and ATTN_TILE & (ATTN_TILE - 1) == 0
    dm = dict(heads=heads, q_rank=q_rank, kv_rank=kv_rank, nope=nope, rope=rope, v_dim=v_dim,
              conv_ch=conv_ch, pool_ch=pool_ch, q_scale=float(qk) ** -0.5 * LOG2_E,
              alpha=float(2 * depth) ** 0.25)

    win_all = _prep_in_weights(w_in, dm)
    wq_all = _prep_q_weights(w_q_b, dm)
    wuk_all = jnp.transpose(w_uk, (0, 2, 3, 1)).astype(BF16)
    wkn_all = w_uk.reshape(depth, kv_rank, heads * nope).astype(BF16)
    wuv_all = jnp.transpose(w_uv, (0, 2, 1, 3)).astype(BF16)
    eye = jnp.eye(n_groups, dtype=pool_w.dtype)
    pw_all = jnp.einsum('lgcd,gh->lgchd', pool_w, eye).reshape(depth, pool_ch, pool_ch).astype(BF16)
    wout_all = w_out.astype(BF16)
    wr_all = jnp.pad(w_router, ((0, 0), (0, 0), (0, LANES - n_exp)))
    wrh_all = wr_all.astype(BF16)
    wrl_all = (wr_all - wrh_all.astype(F32)).astype(BF16)
    br_all = jnp.pad(b_router, ((0, 0), (0, LANES - n_exp)), constant_values=NEG_BIG)[:, None, :]
    cache_kpe_t = jnp.swapaxes(cache_kpe, 2, 3)
    vec = lambda a, l: a[l][None, :]

    cos_tab, sin_tab = _rope_tables(seq, dec_seq, n_s, past, rope)
    tiles_per_seq, prompt_tiles = seq // tm, n_p // tm
    tab_index = lambda i: jnp.where(i < prompt_tiles, lax.rem(i, tiles_per_seq),
                                    tiles_per_seq + i - prompt_tiles)

    weg, weu, wed = w_exp_gate.astype(BF16), w_exp_up.astype(BF16), w_exp_down.astype(BF16)
    xp, xs = x_prompt.reshape(n_p, d_model), x_sample.reshape(n_s, d_model)
    outs = [[] for _ in range(8)]
    for l in range(depth):
        qh, kh, qcat_s, kcat, ckv_p, kpe_p, ckv_s, kpe_s, u, hp = _in_proj(
            xp, xs, win_all[l], wq_all[l], wuk_all[l], wkn_all[l], vec(q_norm_g, l),
            vec(kv_norm_g, l), cos_tab, sin_tab, tab_index, n_p, dm)

        olat_p = _prompt_attn(qh, kh, kcat, batch, seq, dm)
        olat_s = _decode_attn(page_table, qcat_s, kcat, cache_ckv, cache_kpe_t, l,
                              n_p, dec_batch, dec_seq, dm)

        seq_w = (conv_w[l], vec(conv_b, l), vec(conv_ln_g, l), vec(conv_ln_b, l),
                 pw_all[l], vec(pool_scale, l))
        ocp_p, ctail, ptail = _seq_prompt(u, hp, *seq_w, batch, seq, dm)
        cext = jnp.concatenate([state_conv[l], u[n_p:].reshape(dec_batch, dec_seq, conv_ch)], axis=1)
        pext = jnp.concatenate([state_pool[l], hp[n_p:].reshape(dec_batch, dec_seq, pool_ch)], axis=1)
        ocp_s = _seq_sample(jnp.swapaxes(cext, 0, 1), jnp.swapaxes(pext, 0, 1), *seq_w,
                            dec_seq, past, dm)
        ocp_s = jnp.swapaxes(ocp_s, 0, 1).reshape(n_s, conv_ch + pool_ch)

        i = l // 2
        router = (wrh_all[i], wrl_all[i], br_all[i]) if l % 2 == 1 else None
        res = _out_proj(olat_p, olat_s, ocp_p, ocp_s, xp, xs, wuv_all[l], wout_all[l],
                        vec(ln1_g, l), vec(ln1_b, l), router, n_p, dm)
        last = l == depth - 1
        if l % 2 == 0:
            x1, x1b = res
            wg, wu, wd = (w[i].astype(BF16) for w in (w_ffn_gate, w_ffn_up, w_ffn_down))
            x = _ffn_dense(x1b, x1, wg, wu, wd, vec(ln2_g, l), vec(ln2_b, l), n_p, last, dm)
        else:
            x1, x1b, route = res
            x = _moe_layer(x1, x1b, route, weg, weu, wed, i, vec(ln2_g, l), vec(ln2_b, l),
                           n_p, last, dm)
        if not last:
            xp, xs = x, x[n_p:]

        outs[0].append(ckv_p.reshape(batch, seq, kv_rank))
        outs[1].append(kpe_p.reshape(batch, seq, rope))
        outs[2].append(ctail[:, ctail.shape[1] - conv_past:])
        outs[3].append(ptail[:, ptail.shape[1] - pool_past:])
        outs[4].append(ckv_s.reshape(dec_batch, dec_seq, kv_rank))
        outs[5].append(kpe_s.reshape(dec_batch, dec_seq, rope))
        outs[6].append(cext[:, dec_seq:])
        outs[7].append(pext[:, dec_seq:])

    y_p, y_s = x
    return (y_p.reshape(batch, seq, d_model), y_s.reshape(dec_batch, dec_seq, d_model),
            *[jnp.stack(o) for o in outs])
```
